```python
import jax, jax.numpy as jnp
from jax import lax
import numpy as np

D_MODEL = 1024
BATCH = 8
SEQ = 2048
DEPTH = 1
DEC_BATCH = 4
DEC_SEQ = 8192
PAST_LEN = 128

H_RET = 4
DK_RET = 128
DV_RET = 128
RET_CHUNK = 128
H_HG = 4
DK_HG = 128
DV_HG = 128
HG_CHUNK = 64
N_EXPERTS = 16
CAPACITY_FACTOR = 2
D_EXPERT = 2048
ROPE_BASE = 10000.0
NORM_EPS = 1e-6
RET_QK = H_RET * DK_RET
RET_V = H_RET * DV_RET
HG_K = H_HG * DK_HG
HG_V = H_HG * DV_HG
COL_SIZES = [RET_QK, RET_QK, RET_V, RET_V, HG_K, HG_V, HG_K, HG_K, HG_V, D_MODEL, D_MODEL]
D_IN = int(sum(COL_SIZES))

kernel_name = "hybrid_bidir_retention_hgrn2_ec_moe"


def rms_norm(x, g):
    xf = x.astype(jnp.float32)
    y = xf * lax.rsqrt(jnp.mean(xf * xf, axis=-1, keepdims=True) + NORM_EPS)
    return (y * g.astype(jnp.float32)).astype(x.dtype)


def split_cols(u):
    offs = np.cumsum(COL_SIZES)[:-1].tolist()
    return jnp.split(u, offs, axis=-1)


def heads(t, n_heads):
    return t.reshape(t.shape[0], t.shape[1], n_heads, -1)


def flip_seq(t):
    return jnp.flip(t, axis=1)


def to_chunks(t, c):
    b, s, h, d = t.shape
    return t.reshape(b, s // c, c, h, d).transpose(0, 3, 1, 2, 4)


def from_chunks(t):
    b, h, n, c, d = t.shape
    return t.transpose(0, 2, 3, 1, 4).reshape(b, n * c, h, d)


def rotary(x):
    s, hd = x.shape[1], x.shape[-1]
    half = hd // 2
    inv = 1.0 / (ROPE_BASE ** (jnp.arange(half, dtype=jnp.float32) / half))
    ang = jnp.arange(s, dtype=jnp.float32)[:, None] * inv[None, :]
    cos = jnp.cos(ang)[None, :, None, :]
    sin = jnp.sin(ang)[None, :, None, :]
    x1, x2 = x[..., :half], x[..., half:]
    return jnp.concatenate([x1 * cos - x2 * sin, x1 * sin + x2 * cos], axis=-1)


def head_layer_norm(y, g):
    mu = jnp.mean(y, axis=-1, keepdims=True)
    yc = y - mu
    yn = yc * lax.rsqrt(jnp.mean(yc * yc, axis=-1, keepdims=True) + NORM_EPS)
    return yn.reshape(y.shape[0], y.shape[1], -1) * g.astype(jnp.float32)


def head_rms_norm(y, g):
    yn = y * lax.rsqrt(jnp.mean(y * y, axis=-1, keepdims=True) + NORM_EPS)
    return yn.reshape(y.shape[0], y.shape[1], -1) * g.astype(jnp.float32)


def retention_dir(q, k, v, log_gamma, strict):
    c = q.shape[3]
    pos = jnp.arange(c, dtype=jnp.float32)
    diff = pos[:, None] - pos[None, :]
    mask = (diff > 0) if strict else (diff >= 0)
    decay = jnp.where(mask[None], jnp.exp(log_gamma[:, None, None] * jnp.maximum(diff, 0.0)[None]), 0.0)
    scores = jnp.einsum('bhnid,bhnjd->bhnij', q, k) * decay[None, :, None]
    intra = jnp.einsum('bhnij,bhnjv->bhniv', scores, v)
    lg = log_gamma[:, None]
    k_dec = k * jnp.exp(lg * (c - 1 - pos))[None, :, None, :, None]
    t_chunk = jnp.einsum('bhnjd,bhnjv->nbhdv', k_dec, v)
    g_c = jnp.exp(log_gamma * c)[None, :, None, None]

    def step(r, t_n):
        return g_c * r + t_n, r

    _, r_prev = lax.scan(step, jnp.zeros_like(t_chunk[0]), t_chunk)
    q_dec = q * jnp.exp(lg * (pos + 1.0))[None, :, None, :, None]
    cross = jnp.einsum('bhnid,nbhdv->bhniv', q_dec, r_prev)
    return intra + cross


def retention_mixer(q, k, v, g, gn_w):
    qf = rotary(heads(q, H_RET).astype(jnp.float32))
    kf = rotary(heads(k, H_RET).astype(jnp.float32)) * (DK_RET ** -0.5)
    vf = heads(v, H_RET).astype(jnp.float32)
    log_gamma = jnp.log(1.0 - jnp.power(2.0, -5.0 - jnp.arange(H_RET, dtype=jnp.float32)))

    def run(qq, kk, vv, strict):
        return from_chunks(retention_dir(to_chunks(qq, RET_CHUNK), to_chunks(kk, RET_CHUNK),
                                         to_chunks(vv, RET_CHUNK), log_gamma, strict))

    y = run(qf, kf, vf, False) + flip_seq(run(flip_seq(qf), flip_seq(kf), flip_seq(vf), True))
    out = head_layer_norm(y, gn_w) * jax.nn.silu(g.astype(jnp.float32))
    return out.astype(q.dtype)


def gated_chunk_dir(q, k, v, logf):
    c = q.shape[3]
    causal = jnp.tril(jnp.ones((c, c), dtype=bool))

    def step(s, inp):
        qc, kc, vc, lfc = inp
        b = jnp.cumsum(lfc, axis=2)
        diff = b[:, :, :, None, :] - b[:, :, None, :, :]
        w = jnp.where(causal[:, :, None], jnp.exp(jnp.minimum(diff, 0.0)), 0.0)
        a = jnp.einsum('bhijd,bhjd->bhij', w * qc[:, :, :, None, :], kc)
        intra = jnp.einsum('bhij,bhjv->bhiv', a, vc)
        inter = jnp.einsum('bhid,bhdv->bhiv', qc * jnp.exp(b), s)
        b_last = b[:, :, -1:, :]
        s_new = jnp.exp(b_last[:, :, 0, :])[..., None] * s + jnp.einsum('bhjd,bhjv->bhdv', kc * jnp.exp(b_last - b), vc)
        return s_new, intra + inter

    mv = lambda t: jnp.moveaxis(t, 2, 0)
    bsz, h = q.shape[0], q.shape[1]
    s0 = jnp.zeros((bsz, h, q.shape[-1], v.shape[-1]), jnp.float32)
    _, o = lax.scan(step, s0, (mv(q), mv(k), mv(v), mv(logf)))
    return jnp.moveaxis(o, 0, 2)


def hgrn2_mixer(q, i, zf_fwd, zf_bwd, g, lb, gn_w):
    qf = heads(q, H_HG).astype(jnp.float32)
    inp = heads(i, H_HG).astype(jnp.float32)
    lbh = lb.reshape(H_HG, DK_HG)

    def log_forget(z):
        zf = heads(z, H_HG).astype(jnp.float32)
        return jnp.logaddexp(jnp.log(lbh), jnp.log1p(-lbh) + jax.nn.log_sigmoid(zf))

    def run(qq, ii, lf):
        kk = -jnp.expm1(lf)
        o = gated_chunk_dir(to_chunks(qq, HG_CHUNK), to_chunks(kk, HG_CHUNK),
                            to_chunks(ii, HG_CHUNK), to_chunks(lf, HG_CHUNK))
        return from_chunks(o)

    y = run(qf, inp, log_forget(zf_fwd)) + flip_seq(run(flip_seq(qf), flip_seq(inp), flip_seq(log_forget(zf_bwd))))
    out = head_rms_norm(y, gn_w) * jax.nn.silu(g.astype(jnp.float32))
    return out.astype(q.dtype)


def expert_choice_ffn(h, w_router, w_gate, w_up, w_down):
    bsz, s, d = h.shape
    t = h.reshape(bsz * s, d)
    n_tok = bsz * s
    cap = CAPACITY_FACTOR * n_tok // N_EXPERTS
    aff = jax.nn.softmax((t @ w_router).astype(jnp.float32), axis=-1)
    gate, idx = lax.top_k(aff.T, cap)
    xe = t[idx]
    a = jnp.einsum('ecd,edf->ecf', xe, w_gate)
    u = jnp.einsum('ecd,edf->ecf', xe, w_up)
    ye = jnp.einsum('ecf,efd->ecd', jax.nn.silu(a) * u, w_down)
    ye = ye * gate[..., None].astype(ye.dtype)
    out = jnp.zeros_like(t).at[idx.reshape(-1)].add(ye.reshape(-1, d))
    return out.reshape(bsz, s, d)


def encoder_layer(x, c, w_ada, b_ada, g_pre1, g_post1, w_in, gn_ret, gn_hg, lb, w_br_ret, w_br_hg,
                  w_out, g_pre2, g_post2, w_router, w_gate, w_up, w_down):
    mod = jax.nn.silu(c) @ w_ada + b_ada
    sh1, sc1, ga1, sh2, sc2, ga2 = jnp.split(mod[:, None, :], 6, axis=-1)
    h = rms_norm(x, g_pre1) * (1.0 + sc1) + sh1
    u = h @ w_in
    qr, kr, vr, gr, qh, ih, ffh, fbh, gh, za, zb = split_cols(u)
    y_ret = retention_mixer(qr, kr, vr, gr, gn_ret)
    y_hg = hgrn2_mixer(qh, ih, ffh, fbh, gh, lb, gn_hg)
    merged = jax.nn.sigmoid(za) * (y_ret @ w_br_ret) + jax.nn.sigmoid(zb) * (y_hg @ w_br_hg)
    x = x + ga1 * rms_norm(merged @ w_out, g_post1)
    h2 = rms_norm(x, g_pre2) * (1.0 + sc2) + sh2
    x = x + ga2 * rms_norm(expert_choice_ffn(h2, w_router, w_gate, w_up, w_down), g_post2)
    return x


def setup_inputs(seed: int = 0) -> dict:
    key = jax.random.key(seed)
    ks = jax.random.split(key, 24)
    nrm = lambda k, shape, scale: jax.random.normal(k, shape, jnp.float32) * scale
    gain = lambda k, shape: 1.0 + 0.01 * jax.random.normal(k, shape, jnp.float32)
    L, D = DEPTH, D_MODEL
    return {
        "x_prompt": nrm(ks[0], (BATCH, SEQ, D), 1.0),
        "x_sample": nrm(ks[1], (DEC_BATCH, DEC_SEQ, D), 1.0),
        "c_prompt": nrm(ks[2], (BATCH, D), 1.0),
        "c_sample": nrm(ks[3], (DEC_BATCH, D), 1.0),
        "w_ada": nrm(ks[4], (L, D, 6 * D), 0.5 * D ** -0.5),
        "b_ada": nrm(ks[5], (L, 6 * D), 0.01),
        "g_pre1": gain(ks[6], (L, D)),
        "g_post1": gain(ks[7], (L, D)),
        "w_in": nrm(ks[8], (L, D, D_IN), D ** -0.5),
        "gn_ret": gain(ks[9], (L, RET_V)),
        "gn_hg": gain(ks[10], (L, HG_V)),
        "lb_logits": nrm(ks[11], (L + 1, HG_K), 0.5),
        "w_br_ret": nrm(ks[12], (L, RET_V, D), RET_V ** -0.5),
        "w_br_hg": nrm(ks[13], (L, HG_V, D), HG_V ** -0.5),
        "w_out": nrm(ks[14], (L, D, D), D ** -0.5),
        "g_pre2": gain(ks[15], (L, D)),
        "g_post2": gain(ks[16], (L, D)),
        "w_router": nrm(ks[17], (L, D, N_EXPERTS), D ** -0.5),
        "w_gate": nrm(ks[18], (L, N_EXPERTS, D, D_EXPERT), D ** -0.5),
        "w_up": nrm(ks[19], (L, N_EXPERTS, D, D_EXPERT), D ** -0.5),
        "w_down": nrm(ks[20], (L, N_EXPERTS, D_EXPERT, D), D_EXPERT ** -0.5),
    }


def reference(x_prompt, x_sample, c_prompt, c_sample, w_ada, b_ada, g_pre1, g_post1, w_in, gn_ret, gn_hg,
              lb_logits, w_br_ret, w_br_hg, w_out, g_pre2, g_post2, w_router, w_gate, w_up, w_down):
    lower_bounds = jnp.cumsum(jax.nn.softmax(lb_logits.astype(jnp.float32), axis=0), axis=0)
    xp, xs = x_prompt, x_sample
    for l in range(DEPTH):
        params = (w_ada[l], b_ada[l], g_pre1[l], g_post1[l], w_in[l], gn_ret[l], gn_hg[l], lower_bounds[l],
                  w_br_ret[l], w_br_hg[l], w_out[l], g_pre2[l], g_post2[l], w_router[l], w_gate[l], w_up[l], w_down[l])
        xp = encoder_layer(xp, c_prompt, *params)
        xs = encoder_layer(xs, c_sample, *params)
    return (xp, xs)
```

```python
import functools

import numpy as np
import jax
import jax.numpy as jnp
from jax import lax
from jax.experimental import pallas as pl
from jax.experimental.pallas import tpu as pltpu

F32 = jnp.float32
BF16 = jnp.bfloat16

D_MODEL = 1024
N_HEADS = 4
D_HEAD = 128
N_EXPERTS = 16
CAPACITY_FACTOR = 2
D_EXPERT = 2048
ROPE_BASE = 10000.0
NORM_EPS = 1e-6
CHUNK = 128
LANES = 128
VMEM_LIMIT = 56 * 1024 * 1024

D_IN = 2 * D_MODEL + 9 * N_HEADS * D_HEAD
COL_RETQ, COL_RETK, COL_RETV, COL_RETG = 16, 20, 24, 28
COL_HGQ, COL_HGI, COL_HGFF, COL_HGFB, COL_HGG = 32, 36, 40, 44, 48
TN_IN = 512
ROPE_CHUNKS = (COL_RETQ * LANES // TN_IN, COL_RETK * LANES // TN_IN)

NT_DIMS = (((1,), (1,)), ((), ()))
TN_DIMS = (((0,), (0,)), ((), ()))


def _sigmoid(x):
    return 1.0 / (1.0 + jnp.exp(-x))


def _rms(x, eps=NORM_EPS):
    return x * lax.rsqrt(jnp.mean(x * x, axis=-1, keepdims=True) + eps)


def _ada_kernel(c_ref, w_ref, b_ref, o_ref):
    c = c_ref[...]
    o_ref[...] = jnp.dot(c * _sigmoid(c), w_ref[...], preferred_element_type=F32,
                         precision=lax.Precision.HIGHEST) + b_ref[...]


def _ada(c_all, w_ada, b_ada):
    rows = c_all.shape[0]
    n_out = w_ada.shape[1]
    tn = 1024
    return pl.pallas_call(
        _ada_kernel,
        grid=(n_out // tn,),
        in_specs=[pl.BlockSpec((rows, D_MODEL), lambda n: (0, 0)),
                  pl.BlockSpec((D_MODEL, tn), lambda n: (0, n)),
                  pl.BlockSpec((1, tn), lambda n: (0, n))],
        out_specs=pl.BlockSpec((rows, tn), lambda n: (0, n)),
        out_shape=jax.ShapeDtypeStruct((rows, n_out), F32),
        name="ada_mod",
    )(c_all, w_ada, b_ada.reshape(1, n_out))


def _inproj_kernel(x_ref, sc_ref, sh_ref, g_ref, cos_ref, sin_ref, w_ref, u_ref):
    h = (_rms(x_ref[0]) * g_ref[...] * (1.0 + sc_ref[0]) + sh_ref[0]).astype(BF16)
    for n in range(D_IN // TN_IN):
        acc = jnp.dot(h, w_ref[:, n * TN_IN:(n + 1) * TN_IN], preferred_element_type=F32)
        if n in ROPE_CHUNKS:
            scale = 1.0 if n == ROPE_CHUNKS[0] else D_HEAD ** -0.5
            for hd in range(TN_IN // D_HEAD):
                blk = acc[:, hd * D_HEAD:(hd + 1) * D_HEAD]
                rot = blk * cos_ref[...] + pltpu.roll(blk, D_HEAD // 2, axis=1) * sin_ref[...]
                lo = n * TN_IN + hd * D_HEAD
                u_ref[0, :, lo:lo + D_HEAD] = (rot * scale).astype(BF16)
        else:
            u_ref[0, :, n * TN_IN:(n + 1) * TN_IN] = acc.astype(BF16)


def _inproj(x, sc, sh, g, cos2, sin2, w_bf16, tm=512):
    b, s, d = x.shape
    return pl.pallas_call(
        _inproj_kernel,
        grid=(b, s // tm),
        in_specs=[pl.BlockSpec((1, tm, d), lambda i, j: (i, j, 0)),
                  pl.BlockSpec((1, 1, d), lambda i, j: (i, 0, 0)),
                  pl.BlockSpec((1, 1, d), lambda i, j: (i, 0, 0)),
                  pl.BlockSpec((1, d), lambda i, j: (0, 0)),
                  pl.BlockSpec((tm, D_HEAD), lambda i, j: (j, 0)),
                  pl.BlockSpec((tm, D_HEAD), lambda i, j: (j, 0)),
                  pl.BlockSpec((d, D_IN), lambda i, j: (0, 0))],
        out_specs=pl.BlockSpec((1, tm, D_IN), lambda i, j: (i, j, 0)),
        out_shape=jax.ShapeDtypeStruct((b, s, D_IN), BF16),
        compiler_params=pltpu.CompilerParams(dimension_semantics=("parallel", "parallel"),
                                             vmem_limit_bytes=VMEM_LIMIT),
        name="inproj",
    )(x, sc, sh, g.reshape(1, d), cos2, sin2, w_bf16)


def _ret_kernel(q_ref, k_ref, v_ref, g_ref, lg_ref, gn_ref, o_ref, yacc_ref, *, n_chunks):
    c = CHUNK
    lg = lg_ref[0]
    pos = lax.broadcasted_iota(jnp.int32, (c, LANES), 0).astype(F32)
    dist = jnp.abs(lax.broadcasted_iota(jnp.int32, (c, c), 0)
                   - lax.broadcasted_iota(jnp.int32, (c, c), 1)).astype(F32)
    decay = jnp.exp(lg * dist)
    q_fwd = jnp.exp(lg * (pos + 1.0))
    k_fwd = jnp.exp(lg * (c - 1.0 - pos))
    q_bwd = jnp.exp(lg * (c - pos))
    k_bwd = jnp.exp(lg * pos)
    g_chunk = jnp.exp(lg * c)

    def scaled(t, f):
        return (t.astype(F32) * f).astype(BF16)

    def fwd(n, r):
        sl = pl.ds(pl.multiple_of(n * c, c), c)
        q, k, v = q_ref[0, sl, :], k_ref[0, sl, :], v_ref[0, sl, :]
        s = lax.dot_general(q, k, NT_DIMS, preferred_element_type=F32) * decay
        y = jnp.dot(s.astype(BF16), v, preferred_element_type=F32)
        y += jnp.dot(scaled(q, q_fwd), r.astype(BF16), preferred_element_type=F32)
        yacc_ref[sl, :] = y
        return g_chunk * r + lax.dot_general(scaled(k, k_fwd), v, TN_DIMS, preferred_element_type=F32)

    lax.fori_loop(0, n_chunks, fwd, jnp.zeros((D_HEAD, D_HEAD), F32))

    def bwd(i, r):
        sl = pl.ds(pl.multiple_of((n_chunks - 1 - i) * c, c), c)
        q, k, v = q_ref[0, sl, :], k_ref[0, sl, :], v_ref[0, sl, :]
        y = yacc_ref[sl, :] + jnp.dot(scaled(q, q_bwd), r.astype(BF16), preferred_element_type=F32)
        yc = y - jnp.mean(y, axis=-1, keepdims=True)
        g = g_ref[0, sl, :].astype(F32)
        o_ref[0, sl, :] = (_rms(yc) * gn_ref[0] * (g * _sigmoid(g))).astype(BF16)
        return g_chunk * r + lax.dot_general(scaled(k, k_bwd), v, TN_DIMS, preferred_element_type=F32)

    lax.fori_loop(0, n_chunks, bwd, jnp.zeros((D_HEAD, D_HEAD), F32))


def _retention(u, log_gamma, gn):
    b, s, _ = u.shape

    def col(c0):
        return pl.BlockSpec((1, s, D_HEAD), lambda i, h: (i, 0, c0 + h))

    per_head = pl.BlockSpec((1, 1, D_HEAD), lambda i, h: (h, 0, 0))
    return pl.pallas_call(
        functools.partial(_ret_kernel, n_chunks=s // CHUNK),
        grid=(b, N_HEADS),
        in_specs=[col(COL_RETQ), col(COL_RETK), col(COL_RETV), col(COL_RETG), per_head, per_head],
        out_specs=pl.BlockSpec((1, s, D_HEAD), lambda i, h: (i, 0, h)),
        out_shape=jax.ShapeDtypeStruct((b, s, N_HEADS * D_HEAD), BF16),
        scratch_shapes=[pltpu.VMEM((s, D_HEAD), F32)],
        compiler_params=pltpu.CompilerParams(dimension_semantics=("parallel", "parallel"),
                                             vmem_limit_bytes=VMEM_LIMIT),
        name="retention",
    )(u, u, u, u, log_gamma, gn.reshape(N_HEADS, 1, D_HEAD))


N_LEVELS = 8


def _hg_constants():
    c = CHUNK
    p = np.arange(c)
    tril = (p[None, :] <= p[:, None]).astype(np.float64)
    mats, masks = [tril], [np.eye(c)]
    for l in range(1, N_LEVELS):
        s, half = 2 ** l, 2 ** (l - 1)
        blk = p // s
        sel = np.zeros((c, c))
        sel[p, blk * s + half - 1] = 1.0
        mats.append(tril - sel @ tril)
        upper = (p % s) >= half
        masks.append(((blk[:, None] == blk[None, :]) & upper[:, None] & (~upper)[None, :]).astype(np.float64))
    mat_f = np.concatenate(mats, axis=0)
    mat_b = np.concatenate([m[::-1, ::-1] for m in mats], axis=0)
    mask_f = np.stack(masks)
    mask_b = np.stack([m.T for m in masks])
    tile3 = lambda m: np.concatenate([m, m, m], axis=1)
    return (jnp.asarray(tile3(mat_f), BF16), jnp.asarray(tile3(mat_b), BF16),
            jnp.asarray(mask_f, F32), jnp.asarray(mask_b, F32))


def _hg_chunk(q, v, z, lb, state, mat_ref, mask_ref, *, forward):
    c = CHUNK
    sg = _sigmoid(z)
    lf = jnp.log(lb + (1.0 - lb) * sg)
    kk = (1.0 - lb) * (1.0 - sg)
    hi = lf.astype(BF16)
    r1 = lf - hi.astype(F32)
    mid = r1.astype(BF16)
    lo = (r1 - mid.astype(F32)).astype(BF16)
    offs = jnp.dot(mat_ref[...], jnp.concatenate([hi, mid, lo], axis=0), preferred_element_type=F32)
    cum = offs[0:c]
    qf = q.astype(F32)
    row = lax.broadcasted_iota(jnp.int32, (c, LANES), 0)
    a = lax.dot_general(q, kk.astype(BF16), NT_DIMS, preferred_element_type=F32) * mask_ref[0]
    for l in range(1, N_LEVELS):
        e = jnp.exp(-jnp.abs(offs[l * c:(l + 1) * c]))
        in_upper = (row & (2 ** (l - 1))) != 0
        is_query = in_upper if forward else jnp.logical_not(in_upper)
        ql = jnp.where(is_query, qf * e, 0.0).astype(BF16)
        kl = jnp.where(is_query, 0.0, kk * e).astype(BF16)
        a += lax.dot_general(ql, kl, NT_DIMS, preferred_element_type=F32) * mask_ref[l]
    y = jnp.dot(a.astype(BF16), v, preferred_element_type=F32)
    y += lax.dot_general((qf * jnp.exp(cum)).astype(BF16), state.astype(BF16), NT_DIMS,
                         preferred_element_type=F32)
    total = cum[c - 1:c] if forward else cum[0:1]
    kd = (kk * jnp.exp(total - cum)).astype(BF16)
    new_state = state * jnp.exp(total) + lax.dot_general(v, kd, TN_DIMS, preferred_element_type=F32)
    return y, new_state


def _hg_kernel(q_ref, i_ref, zf_ref, zb_ref, g_ref, lb_ref, gn_ref, matf_ref, matb_ref, maskf_ref, maskb_ref,
               o_ref, yacc_ref, *, n_chunks):
    c = CHUNK
    lb = lb_ref[0]

    def fwd(n, state):
        sl = pl.ds(pl.multiple_of(n * c, c), c)
        y, state = _hg_chunk(q_ref[0, sl, :], i_ref[0, sl, :], zf_ref[0, sl, :].astype(F32), lb, state,
                             matf_ref, maskf_ref, forward=True)
        yacc_ref[sl, :] = y
        return state

    lax.fori_loop(0, n_chunks, fwd, jnp.zeros((D_HEAD, D_HEAD), F32))

    def bwd(i, state):
        sl = pl.ds(pl.multiple_of((n_chunks - 1 - i) * c, c), c)
        y, state = _hg_chunk(q_ref[0, sl, :], i_ref[0, sl, :], zb_ref[0, sl, :].astype(F32), lb, state,
                             matb_ref, maskb_ref, forward=False)
        y += yacc_ref[sl, :]
        g = g_ref[0, sl, :].astype(F32)
        o_ref[0, sl, :] = (_rms(y) * gn_ref[0] * (g * _sigmoid(g))).astype(BF16)
        return state

    lax.fori_loop(0, n_chunks, bwd, jnp.zeros((D_HEAD, D_HEAD), F32))


def _hgrn2(u, lb, gn, consts):
    b, s, _ = u.shape
    mat_f, mat_b, mask_f, mask_b = consts

    def col(c0):
        return pl.BlockSpec((1, s, D_HEAD), lambda i, h: (i, 0, c0 + h))

    def whole(a):
        return pl.BlockSpec(a.shape, lambda i, h: (0,) * a.ndim)

    per_head = pl.BlockSpec((1, 1, D_HEAD), lambda i, h: (h, 0, 0))
    return pl.pallas_call(
        functools.partial(_hg_kernel, n_chunks=s // CHUNK),
        grid=(b, N_HEADS),
        in_specs=[col(COL_HGQ), col(COL_HGI), col(COL_HGFF), col(COL_HGFB), col(COL_HGG), per_head, per_head,
                  whole(mat_f), whole(mat_b), whole(mask_f), whole(mask_b)],
        out_specs=pl.BlockSpec((1, s, D_HEAD), lambda i, h: (i, 0, h)),
        out_shape=jax.ShapeDtypeStruct((b, s, N_HEADS * D_HEAD), BF16),
        scratch_shapes=[pltpu.VMEM((s, D_HEAD), F32)],
        compiler_params=pltpu.CompilerParams(dimension_semantics=("parallel", "parallel"),
                                             vmem_limit_bytes=VMEM_LIMIT),
        name="hgrn2",
    )(u, u, u, u, u, lb.reshape(N_HEADS, 1, D_HEAD), gn.reshape(N_HEADS, 1, D_HEAD),
      mat_f, mat_b, mask_f, mask_b)


def _merge_kernel(x_ref, yr_ref, yh_ref, za_ref, zb_ref, ga1_ref, sc2_ref, sh2_ref, gpost1_ref, gpre2_ref,
                  wr_ref, wh_ref, wo_ref, wrt_ref, x1_ref, h2_ref, aff_ref):
    br = jnp.dot(yr_ref[0], wr_ref[...], preferred_element_type=F32)
    bh = jnp.dot(yh_ref[0], wh_ref[...], preferred_element_type=F32)
    merged = _sigmoid(za_ref[0].astype(F32)) * br + _sigmoid(zb_ref[0].astype(F32)) * bh
    o = jnp.dot(merged.astype(BF16), wo_ref[...], preferred_element_type=F32)
    x1 = x_ref[0] + ga1_ref[0] * (_rms(o) * gpost1_ref[...])
    x1_ref[0] = x1
    h2 = _rms(x1) * gpre2_ref[...] * (1.0 + sc2_ref[0]) + sh2_ref[0]
    h2_ref[0] = h2.astype(BF16)
    logits = lax.dot_general(wrt_ref[...], h2, NT_DIMS, preferred_element_type=F32,
                             precision=lax.Precision.HIGHEST)
    ex = jnp.exp(logits - jnp.max(logits, axis=0, keepdims=True))
    aff_ref[...] = ex / jnp.sum(ex, axis=0, keepdims=True)


def _merge(x, y_ret, y_hg, u, ga1, sc2, sh2, g_post1, g_pre2, w_br_ret, w_br_hg, w_out, w_router_t, tm=512):
    b, s, d = x.shape
    nj = s // tm
    tok = lambda w: pl.BlockSpec((1, tm, w), lambda i, j: (i, j, 0))
    per_b = pl.BlockSpec((1, 1, d), lambda i, j: (i, 0, 0))
    vec = pl.BlockSpec((1, d), lambda i, j: (0, 0))
    whole = lambda a: pl.BlockSpec(a.shape, lambda i, j: (0,) * a.ndim)
    return pl.pallas_call(
        _merge_kernel,
        grid=(b, nj),
        in_specs=[tok(d), tok(N_HEADS * D_HEAD), tok(N_HEADS * D_HEAD),
                  pl.BlockSpec((1, tm, d), lambda i, j: (i, j, 0)),
                  pl.BlockSpec((1, tm, d), lambda i, j: (i, j, 1)),
                  per_b, per_b, per_b, vec, vec,
                  whole(w_br_ret), whole(w_br_hg), whole(w_out), whole(w_router_t)],
        out_specs=[tok(d), tok(d), pl.BlockSpec((N_EXPERTS, tm), lambda i, j: (0, i * nj + j))],
        out_shape=[jax.ShapeDtypeStruct((b, s, d), F32), jax.ShapeDtypeStruct((b, s, d), BF16),
                   jax.ShapeDtypeStruct((N_EXPERTS, b * s), F32)],
        compiler_params=pltpu.CompilerParams(dimension_semantics=("parallel", "parallel"),
                                             vmem_limit_bytes=VMEM_LIMIT),
        name="merge_outproj_router",
    )(x, y_ret, y_hg, u, u, ga1, sc2, sh2, g_post1.reshape(1, d), g_pre2.reshape(1, d),
      w_br_ret, w_br_hg, w_out, w_router_t)


TF_EXPERT = 512


def _expert_kernel(x_ref, gate_ref, wg_ref, wu_ref, wd_ref, o_ref):
    x = x_ref[0]
    acc = jnp.zeros((x.shape[0], D_MODEL), F32)
    for f in range(D_EXPERT // TF_EXPERT):
        sl = slice(f * TF_EXPERT, (f + 1) * TF_EXPERT)
        a = jnp.dot(x, wg_ref[0, :, sl], preferred_element_type=F32)
        up = jnp.dot(x, wu_ref[0, :, sl], preferred_element_type=F32)
        acc += jnp.dot((a * _sigmoid(a) * up).astype(BF16), wd_ref[0, sl, :], preferred_element_type=F32)
    o_ref[0] = (acc * gate_ref[0]).astype(BF16)


def _experts(xe, gate, wg, wu, wd, tm=512):
    e, cap, d = xe.shape
    return pl.pallas_call(
        _expert_kernel,
        grid=(e, cap // tm),
        in_specs=[pl.BlockSpec((1, tm, d), lambda i, j: (i, j, 0)),
                  pl.BlockSpec((1, tm, 1), lambda i, j: (i, j, 0)),
                  pl.BlockSpec((1, d, D_EXPERT), lambda i, j: (i, 0, 0)),
                  pl.BlockSpec((1, d, D_EXPERT), lambda i, j: (i, 0, 0)),
                  pl.BlockSpec((1, D_EXPERT, d), lambda i, j: (i, 0, 0))],
        out_specs=pl.BlockSpec((1, tm, d), lambda i, j: (i, j, 0)),
        out_shape=jax.ShapeDtypeStruct((e, cap, d), BF16),
        compiler_params=pltpu.CompilerParams(dimension_semantics=("parallel", "parallel"),
                                             vmem_limit_bytes=VMEM_LIMIT),
        name="expert_ffn",
    )(xe, gate.reshape(e, cap, 1), wg, wu, wd)


def _final_kernel(x1_ref, m_ref, ga2_ref, g_ref, o_ref):
    o_ref[0] = x1_ref[0] + ga2_ref[0] * (_rms(m_ref[0]) * g_ref[...])


def _final(x1, moe, ga2, g_post2, tm=512):
    b, s, d = x1.shape
    tok = pl.BlockSpec((1, tm, d), lambda i, j: (i, j, 0))
    return pl.pallas_call(
        _final_kernel,
        grid=(b, s // tm),
        in_specs=[tok, tok, pl.BlockSpec((1, 1, d), lambda i, j: (i, 0, 0)),
                  pl.BlockSpec((1, d), lambda i, j: (0, 0))],
        out_specs=tok,
        out_shape=jax.ShapeDtypeStruct((b, s, d), F32),
        compiler_params=pltpu.CompilerParams(dimension_semantics=("parallel", "parallel")),
        name="final_norm",
    )(x1, moe, ga2, g_post2.reshape(1, d))


def _rope_tables(s):
    half = D_HEAD // 2
    inv = 1.0 / (ROPE_BASE ** (jnp.arange(half, dtype=F32) / half))
    ang = jnp.arange(s, dtype=F32)[:, None] * inv[None, :]
    cos, sin = jnp.cos(ang), jnp.sin(ang)
    return jnp.concatenate([cos, cos], axis=-1), jnp.concatenate([-sin, sin], axis=-1)


def _layer(x, mod, p):
    b, s, d = x.shape
    sh1, sc1, ga1, sh2, sc2, ga2 = [m[:, None, :] for m in jnp.split(mod, 6, axis=-1)]
    cos2, sin2 = _rope_tables(s)
    u = _inproj(x, sc1, sh1, p["g_pre1"], cos2, sin2, p["w_in"])
    y_ret = _retention(u, p["log_gamma"], p["gn_ret"])
    y_hg = _hgrn2(u, p["lb"], p["gn_hg"], p["hg_consts"])
    x1, h2, aff_t = _merge(x, y_ret, y_hg, u, ga1, sc2, sh2, p["g_post1"], p["g_pre2"],
                           p["w_br_ret"], p["w_br_hg"], p["w_out"], p["w_router_t"])
    n_tok = b * s
    cap = CAPACITY_FACTOR * n_tok // N_EXPERTS
    gate, idx = lax.top_k(aff_t, cap)
    xe = jnp.take(h2.reshape(n_tok, d), idx, axis=0)
    ye = _experts(xe, gate, p["w_gate"], p["w_up"], p["w_down"])
    moe = jnp.zeros((n_tok, d), F32).at[idx.reshape(-1)].add(ye.reshape(-1, d).astype(F32))
    return _final(x1, moe.reshape(b, s, d), ga2, p["g_post2"])


def kernel(x_prompt, x_sample, c_prompt, c_sample, w_ada, b_ada, g_pre1, g_post1, w_in, gn_ret, gn_hg, lb_logits, w_br_ret, w_br_hg, w_out, g_pre2, g_post2, w_router, w_gate, w_up, w_down):
    lower_bounds = jnp.cumsum(jax.nn.softmax(lb_logits.astype(F32), axis=0), axis=0)
    log_gamma = jnp.log(1.0 - jnp.power(2.0, -5.0 - jnp.arange(N_HEADS, dtype=F32)))
    consts = _hg_constants()
    n_gate = 2 * D_MODEL
    xp, xs = x_prompt, x_sample
    bp, bs = xp.shape[0], xs.shape[0]
    for l in range(w_ada.shape[0]):
        c_all = jnp.concatenate([c_prompt, c_sample, jnp.zeros((-(bp + bs) % 8, D_MODEL), F32)], axis=0)
        mod = _ada(c_all, w_ada[l], b_ada[l])
        w_in_l = w_in[l]
        p = dict(
            g_pre1=g_pre1[l], g_post1=g_post1[l], g_pre2=g_pre2[l], g_post2=g_post2[l],
            w_in=jnp.concatenate([w_in_l[:, -n_gate:], w_in_l[:, :-n_gate]], axis=1).astype(BF16),
            log_gamma=jnp.broadcast_to(log_gamma[:, None, None], (N_HEADS, 1, LANES)),
            gn_ret=gn_ret[l], gn_hg=gn_hg[l], lb=lower_bounds[l], hg_consts=consts,
            w_br_ret=w_br_ret[l].astype(BF16), w_br_hg=w_br_hg[l].astype(BF16), w_out=w_out[l].astype(BF16),
            w_router_t=w_router[l].T,
            w_gate=w_gate[l].astype(BF16), w_up=w_up[l].astype(BF16), w_down=w_down[l].astype(BF16),
        )
        xp = _layer(xp, mod[:bp], p)
        xs = _layer(xs, mod[bp:bp + bs], p)
    return (xp, xs)
```

```python
import functools

import numpy as np
import jax
import jax.numpy as jnp
from jax import lax
from jax.experimental import pallas as pl
from jax.experimental.pallas import tpu as pltpu

F32 = jnp.float32
BF16 = jnp.bfloat16

D_MODEL = 1024
N_HEADS = 4
D_HEAD = 128
N_EXPERTS = 16
CAPACITY_FACTOR = 2
D_EXPERT = 2048
ROPE_BASE = 10000.0
NORM_EPS = 1e-6
CHUNK = 128
LANES = 128
VMEM_LIMIT = 56 * 1024 * 1024

D_IN = 2 * D_MODEL + 9 * N_HEADS * D_HEAD
COL_RETQ, COL_RETK, COL_RETV, COL_RETG = 16, 20, 24, 28
COL_HGQ, COL_HGI, COL_HGFF, COL_HGFB, COL_HGG = 32, 36, 40, 44, 48
TN_IN = 512
ROPE_CHUNKS = (COL_RETQ * LANES // TN_IN, COL_RETK * LANES // TN_IN)

NT_DIMS = (((1,), (1,)), ((), ()))
TN_DIMS = (((0,), (0,)), ((), ()))


def _sigmoid(x):
    return 1.0 / (1.0 + jnp.exp(-x))


def _rms(x, eps=NORM_EPS):
    return x * lax.rsqrt(jnp.mean(x * x, axis=-1, keepdims=True) + eps)


def _ada_kernel(c_ref, w_ref, b_ref, o_ref):
    c = c_ref[...]
    o_ref[...] = jnp.dot(c * _sigmoid(c), w_ref[...], preferred_element_type=F32,
                         precision=lax.Precision.HIGHEST) + b_ref[...]


def _ada(c_all, w_ada, b_ada):
    rows = c_all.shape[0]
    n_out = w_ada.shape[1]
    tn = 1024
    return pl.pallas_call(
        _ada_kernel,
        grid=(n_out // tn,),
        in_specs=[pl.BlockSpec((rows, D_MODEL), lambda n: (0, 0)),
                  pl.BlockSpec((D_MODEL, tn), lambda n: (0, n)),
                  pl.BlockSpec((1, tn), lambda n: (0, n))],
        out_specs=pl.BlockSpec((rows, tn), lambda n: (0, n)),
        out_shape=jax.ShapeDtypeStruct((rows, n_out), F32),
        name="ada_mod",
    )(c_all, w_ada, b_ada.reshape(1, n_out))


def _inproj_kernel(x_ref, sc_ref, sh_ref, g_ref, cos_ref, sin_ref, w_ref, u_ref):
    h = (_rms(x_ref[0]) * g_ref[...] * (1.0 + sc_ref[0]) + sh_ref[0]).astype(BF16)
    for n in range(D_IN // TN_IN):
        acc = jnp.dot(h, w_ref[:, n * TN_IN:(n + 1) * TN_IN], preferred_element_type=F32)
        if n in ROPE_CHUNKS:
            scale = 1.0 if n == ROPE_CHUNKS[0] else D_HEAD ** -0.5
            for hd in range(TN_IN // D_HEAD):
                blk = acc[:, hd * D_HEAD:(hd + 1) * D_HEAD]
                rot = blk * cos_ref[...] + pltpu.roll(blk, D_HEAD // 2, axis=1) * sin_ref[...]
                lo = n * TN_IN + hd * D_HEAD
                u_ref[0, :, lo:lo + D_HEAD] = (rot * scale).astype(BF16)
        else:
            u_ref[0, :, n * TN_IN:(n + 1) * TN_IN] = acc.astype(BF16)


def _inproj(x, sc, sh, g, cos2, sin2, w_bf16, tm=512):
    b, s, d = x.shape
    return pl.pallas_call(
        _inproj_kernel,
        grid=(b, s // tm),
        in_specs=[pl.BlockSpec((1, tm, d), lambda i, j: (i, j, 0)),
                  pl.BlockSpec((1, 1, d), lambda i, j: (i, 0, 0)),
                  pl.BlockSpec((1, 1, d), lambda i, j: (i, 0, 0)),
                  pl.BlockSpec((1, d), lambda i, j: (0, 0)),
                  pl.BlockSpec((tm, D_HEAD), lambda i, j: (j, 0)),
                  pl.BlockSpec((tm, D_HEAD), lambda i, j: (j, 0)),
                  pl.BlockSpec((d, D_IN), lambda i, j: (0, 0))],
        out_specs=pl.BlockSpec((1, tm, D_IN), lambda i, j: (i, j, 0)),
        out_shape=jax.ShapeDtypeStruct((b, s, D_IN), BF16),
        compiler_params=pltpu.CompilerParams(dimension_semantics=("parallel", "parallel"),
                                             vmem_limit_bytes=VMEM_LIMIT),
        name="inproj",
    )(x, sc, sh, g.reshape(1, d), cos2, sin2, w_bf16)


def _ret_kernel(q_ref, k_ref, v_ref, g_ref, lg_ref, gn_ref, o_ref, yacc_ref, *, n_chunks):
    c = CHUNK
    lg = lg_ref[0]
    pos = lax.broadcasted_iota(jnp.int32, (c, LANES), 0).astype(F32)
    dist = jnp.abs(lax.broadcasted_iota(jnp.int32, (c, c), 0)
                   - lax.broadcasted_iota(jnp.int32, (c, c), 1)).astype(F32)
    decay = jnp.exp(lg * dist)
    q_fwd = jnp.exp(lg * (pos + 1.0))
    k_fwd = jnp.exp(lg * (c - 1.0 - pos))
    q_bwd = jnp.exp(lg * (c - pos))
    k_bwd = jnp.exp(lg * pos)
    g_chunk = jnp.exp(lg * c)

    def scaled(t, f):
        return (t.astype(F32) * f).astype(BF16)

    def fwd(n, r):
        sl = pl.ds(pl.multiple_of(n * c, c), c)
        q, k, v = q_ref[0, sl, :], k_ref[0, sl, :], v_ref[0, sl, :]
        s = lax.dot_general(q, k, NT_DIMS, preferred_element_type=F32) * decay
        y = jnp.dot(s.astype(BF16), v, preferred_element_type=F32)
        y += jnp.dot(scaled(q, q_fwd), r.astype(BF16), preferred_element_type=F32)
        yacc_ref[sl, :] = y
        return g_chunk * r + lax.dot_general(scaled(k, k_fwd), v, TN_DIMS, preferred_element_type=F32)

    lax.fori_loop(0, n_chunks, fwd, jnp.zeros((D_HEAD, D_HEAD), F32))

    def bwd(i, r):
        sl = pl.ds(pl.multiple_of((n_chunks - 1 - i) * c, c), c)
        q, k, v = q_ref[0, sl, :], k_ref[0, sl, :], v_ref[0, sl, :]
        y = yacc_ref[sl, :] + jnp.dot(scaled(q, q_bwd), r.astype(BF16), preferred_element_type=F32)
        yc = y - jnp.mean(y, axis=-1, keepdims=True)
        g = g_ref[0, sl, :].astype(F32)
        o_ref[0, sl, :] = (_rms(yc) * gn_ref[0] * (g * _sigmoid(g))).astype(BF16)
        return g_chunk * r + lax.dot_general(scaled(k, k_bwd), v, TN_DIMS, preferred_element_type=F32)

    lax.fori_loop(0, n_chunks, bwd, jnp.zeros((D_HEAD, D_HEAD), F32))


def _retention(u, log_gamma, gn):
    b, s, _ = u.shape

    def col(c0):
        return pl.BlockSpec((1, s, D_HEAD), lambda i, h: (i, 0, c0 + h))

    per_head = pl.BlockSpec((1, 1, D_HEAD), lambda i, h: (h, 0, 0))
    return pl.pallas_call(
        functools.partial(_ret_kernel, n_chunks=s // CHUNK),
        grid=(b, N_HEADS),
        in_specs=[col(COL_RETQ), col(COL_RETK), col(COL_RETV), col(COL_RETG), per_head, per_head],
        out_specs=pl.BlockSpec((1, s, D_HEAD), lambda i, h: (i, 0, h)),
        out_shape=jax.ShapeDtypeStruct((b, s, N_HEADS * D_HEAD), BF16),
        scratch_shapes=[pltpu.VMEM((s, D_HEAD), F32)],
        compiler_params=pltpu.CompilerParams(dimension_semantics=("parallel", "parallel"),
                                             vmem_limit_bytes=VMEM_LIMIT),
        name="retention",
    )(u, u, u, u, log_gamma, gn.reshape(N_HEADS, 1, D_HEAD))


N_LEVELS = 8


def _hg_constants():
    c = CHUNK
    p = np.arange(c)
    tril = (p[None, :] <= p[:, None]).astype(np.float64)
    mats, masks = [tril], [np.eye(c)]
    for l in range(1, N_LEVELS):
        s, half = 2 ** l, 2 ** (l - 1)
        blk = p // s
        sel = np.zeros((c, c))
        sel[p, blk * s + half - 1] = 1.0
        mats.append(tril - sel @ tril)
        upper = (p % s) >= half
        masks.append(((blk[:, None] == blk[None, :]) & upper[:, None] & (~upper)[None, :]).astype(np.float64))
    mat_f = np.concatenate(mats, axis=0)
    mat_b = np.concatenate([m[::-1, ::-1] for m in mats], axis=0)
    mask_f = np.stack(masks)
    mask_b = np.stack([m.T for m in masks])
    tile3 = lambda m: np.concatenate([m, m, m], axis=1)
    return (jnp.asarray(tile3(mat_f), BF16), jnp.asarray(tile3(mat_b), BF16),
            jnp.asarray(mask_f, F32), jnp.asarray(mask_b, F32))


def _hg_chunk(q, v, z, lb, state, mat_ref, mask_ref, *, forward):
    c = CHUNK
    sg = _sigmoid(z)
    lf = jnp.log(lb + (1.0 - lb) * sg)
    kk = (1.0 - lb) * (1.0 - sg)
    hi = lf.astype(BF16)
    r1 = lf - hi.astype(F32)
    mid = r1.astype(BF16)
    lo = (r1 - mid.astype(F32)).astype(BF16)
    offs = jnp.dot(mat_ref[...], jnp.concatenate([hi, mid, lo], axis=0), preferred_element_type=F32)
    cum = offs[0:c]
    qf = q.astype(F32)
    row = lax.broadcasted_iota(jnp.int32, (c, LANES), 0)
    a = lax.dot_general(q, kk.astype(BF16), NT_DIMS, preferred_element_type=F32) * mask_ref[0]
    for l in range(1, N_LEVELS):
        e = jnp.exp(-jnp.abs(offs[l * c:(l + 1) * c]))
        in_upper = (row & (2 ** (l - 1))) != 0
        is_query = in_upper if forward else jnp.logical_not(in_upper)
        ql = jnp.where(is_query, qf * e, 0.0).astype(BF16)
        kl = jnp.where(is_query, 0.0, kk * e).astype(BF16)
        a += lax.dot_general(ql, kl, NT_DIMS, preferred_element_type=F32) * mask_ref[l]
    y = jnp.dot(a.astype(BF16), v, preferred_element_type=F32)
    y += lax.dot_general((qf * jnp.exp(cum)).astype(BF16), state.astype(BF16), NT_DIMS,
                         preferred_element_type=F32)
    total = cum[c - 1:c] if forward else cum[0:1]
    kd = (kk * jnp.exp(total - cum)).astype(BF16)
    new_state = state * jnp.exp(total) + lax.dot_general(v, kd, TN_DIMS, preferred_element_type=F32)
    return y, new_state


def _hg_kernel(q_ref, i_ref, zf_ref, zb_ref, g_ref, lb_ref, gn_ref, matf_ref, matb_ref, maskf_ref, maskb_ref,
               o_ref, yacc_ref, *, n_chunks):
    c = CHUNK
    lb = lb_ref[0]

    def fwd(n, state):
        sl = pl.ds(pl.multiple_of(n * c, c), c)
        y, state = _hg_chunk(q_ref[0, sl, :], i_ref[0, sl, :], zf_ref[0, sl, :].astype(F32), lb, state,
                             matf_ref, maskf_ref, forward=True)
        yacc_ref[sl, :] = y
        return state

    lax.fori_loop(0, n_chunks, fwd, jnp.zeros((D_HEAD, D_HEAD), F32))

    def bwd(i, state):
        sl = pl.ds(pl.multiple_of((n_chunks - 1 - i) * c, c), c)
        y, state = _hg_chunk(q_ref[0, sl, :], i_ref[0, sl, :], zb_ref[0, sl, :].astype(F32), lb, state,
                             matb_ref, maskb_ref, forward=False)
        y += yacc_ref[sl, :]
        g = g_ref[0, sl, :].astype(F32)
        o_ref[0, sl, :] = (_rms(y) * gn_ref[0] * (g * _sigmoid(g))).astype(BF16)
        return state

    lax.fori_loop(0, n_chunks, bwd, jnp.zeros((D_HEAD, D_HEAD), F32))


def _hgrn2(u, lb, gn, consts):
    b, s, _ = u.shape
    mat_f, mat_b, mask_f, mask_b = consts

    def col(c0):
        return pl.BlockSpec((1, s, D_HEAD), lambda i, h: (i, 0, c0 + h))

    def whole(a):
        return pl.BlockSpec(a.shape, lambda i, h: (0,) * a.ndim)

    per_head = pl.BlockSpec((1, 1, D_HEAD), lambda i, h: (h, 0, 0))
    return pl.pallas_call(
        functools.partial(_hg_kernel, n_chunks=s // CHUNK),
        grid=(b, N_HEADS),
        in_specs=[col(COL_HGQ), col(COL_HGI), col(COL_HGFF), col(COL_HGFB), col(COL_HGG), per_head, per_head,
                  whole(mat_f), whole(mat_b), whole(mask_f), whole(mask_b)],
        out_specs=pl.BlockSpec((1, s, D_HEAD), lambda i, h: (i, 0, h)),
        out_shape=jax.ShapeDtypeStruct((b, s, N_HEADS * D_HEAD), BF16),
        scratch_shapes=[pltpu.VMEM((s, D_HEAD), F32)],
        compiler_params=pltpu.CompilerParams(dimension_semantics=("parallel", "parallel"),
                                             vmem_limit_bytes=VMEM_LIMIT),
        name="hgrn2",
    )(u, u, u, u, u, lb.reshape(N_HEADS, 1, D_HEAD), gn.reshape(N_HEADS, 1, D_HEAD),
      mat_f, mat_b, mask_f, mask_b)


def _merge_kernel(x_ref, yr_ref, yh_ref, za_ref, zb_ref, ga1_ref, sc2_ref, sh2_ref, gpost1_ref, gpre2_ref,
                  wr_ref, wh_ref, wo_ref, wrt_ref, x1_ref, h2_ref, aff_ref):
    br = jnp.dot(yr_ref[0], wr_ref[...], preferred_element_type=F32)
    bh = jnp.dot(yh_ref[0], wh_ref[...], preferred_element_type=F32)
    merged = _sigmoid(za_ref[0].astype(F32)) * br + _sigmoid(zb_ref[0].astype(F32)) * bh
    o = jnp.dot(merged.astype(BF16), wo_ref[...], preferred_element_type=F32)
    x1 = x_ref[0] + ga1_ref[0] * (_rms(o) * gpost1_ref[...])
    x1_ref[0] = x1
    h2 = _rms(x1) * gpre2_ref[...] * (1.0 + sc2_ref[0]) + sh2_ref[0]
    h2_ref[0] = h2.astype(BF16)
    logits = lax.dot_general(wrt_ref[...], h2, NT_DIMS, preferred_element_type=F32,
                             precision=lax.Precision.HIGHEST)
    ex = jnp.exp(logits - jnp.max(logits, axis=0, keepdims=True))
    aff_ref[...] = ex / jnp.sum(ex, axis=0, keepdims=True)


def _merge(x, y_ret, y_hg, u, ga1, sc2, sh2, g_post1, g_pre2, w_br_ret, w_br_hg, w_out, w_router_t, tm=512):
    b, s, d = x.shape
    nj = s // tm
    tok = lambda w: pl.BlockSpec((1, tm, w), lambda i, j: (i, j, 0))
    per_b = pl.BlockSpec((1, 1, d), lambda i, j: (i, 0, 0))
    vec = pl.BlockSpec((1, d), lambda i, j: (0, 0))
    whole = lambda a: pl.BlockSpec(a.shape, lambda i, j: (0,) * a.ndim)
    return pl.pallas_call(
        _merge_kernel,
        grid=(b, nj),
        in_specs=[tok(d), tok(N_HEADS * D_HEAD), tok(N_HEADS * D_HEAD),
                  pl.BlockSpec((1, tm, d), lambda i, j: (i, j, 0)),
                  pl.BlockSpec((1, tm, d), lambda i, j: (i, j, 1)),
                  per_b, per_b, per_b, vec, vec,
                  whole(w_br_ret), whole(w_br_hg), whole(w_out), whole(w_router_t)],
        out_specs=[tok(d), tok(d), pl.BlockSpec((N_EXPERTS, tm), lambda i, j: (0, i * nj + j))],
        out_shape=[jax.ShapeDtypeStruct((b, s, d), F32), jax.ShapeDtypeStruct((b, s, d), BF16),
                   jax.ShapeDtypeStruct((N_EXPERTS, b * s), F32)],
        compiler_params=pltpu.CompilerParams(dimension_semantics=("parallel", "parallel"),
                                             vmem_limit_bytes=VMEM_LIMIT),
        name="merge_outproj_router",
    )(x, y_ret, y_hg, u, u, ga1, sc2, sh2, g_post1.reshape(1, d), g_pre2.reshape(1, d),
      w_br_ret, w_br_hg, w_out, w_router_t)


TF_EXPERT = 512


def _expert_kernel(total_ref, x_ref, wg_ref, wu_ref, wd_ref, o_ref, *, tm):
    live = pl.program_id(1) * tm < total_ref[pl.program_id(0)]

    @pl.when(live)
    def _():
        x = x_ref[0]
        acc = jnp.zeros((tm, D_MODEL), F32)
        for f in range(D_EXPERT // TF_EXPERT):
            sl = slice(f * TF_EXPERT, (f + 1) * TF_EXPERT)
            a = jnp.dot(x, wg_ref[0, :, sl], preferred_element_type=F32)
            up = jnp.dot(x, wu_ref[0, :, sl], preferred_element_type=F32)
            acc += jnp.dot((a * _sigmoid(a) * up).astype(BF16), wd_ref[0, sl, :], preferred_element_type=F32)
        o_ref[0] = acc.astype(BF16)

    @pl.when(jnp.logical_not(live))
    def _():
        o_ref[0] = jnp.zeros((tm, D_MODEL), BF16)


def _experts(total, xe, wg, wu, wd, tm=512):
    e, rows, d = xe.shape
    return pl.pallas_call(
        functools.partial(_expert_kernel, tm=tm),
        grid_spec=pltpu.PrefetchScalarGridSpec(
            num_scalar_prefetch=1,
            grid=(e, rows // tm),
            in_specs=[pl.BlockSpec((1, tm, d), lambda i, j, t: (i, j, 0)),
                      pl.BlockSpec((1, d, D_EXPERT), lambda i, j, t: (i, 0, 0)),
                      pl.BlockSpec((1, d, D_EXPERT), lambda i, j, t: (i, 0, 0)),
                      pl.BlockSpec((1, D_EXPERT, d), lambda i, j, t: (i, 0, 0))],
            out_specs=pl.BlockSpec((1, tm, d), lambda i, j, t: (i, j, 0))),
        out_shape=jax.ShapeDtypeStruct((e, rows, d), BF16),
        compiler_params=pltpu.CompilerParams(dimension_semantics=("parallel", "parallel"),
                                             vmem_limit_bytes=VMEM_LIMIT),
        name="expert_ffn",
    )(total, xe, wg, wu, wd)


TT = 512
WIN = 128
SEG_ALIGN = 16


def _threshold_kernel(a_ref, o_ref, *, cap):
    def step(k, v):
        cand = v | jnp.left_shift(jnp.int32(1), 30 - k)
        bits = pltpu.bitcast(a_ref[...], jnp.int32)
        n_ge = jnp.sum(jnp.where(bits >= cand, 1, 0), axis=1, keepdims=True)
        return jnp.where(n_ge >= cap, cand, v)

    thr = lax.fori_loop(0, 31, step, jnp.zeros((N_EXPERTS, 1), jnp.int32))
    bits = pltpu.bitcast(a_ref[...], jnp.int32)
    n_gt = jnp.sum(jnp.where(bits > thr, 1, 0), axis=1, keepdims=True)
    o_ref[0] = jnp.broadcast_to(thr, (N_EXPERTS, LANES))
    o_ref[1] = jnp.broadcast_to(cap - n_gt, (N_EXPERTS, LANES))


def _thresholds(aff_t, cap):
    return pl.pallas_call(
        functools.partial(_threshold_kernel, cap=cap),
        out_shape=jax.ShapeDtypeStruct((2, N_EXPERTS, LANES), jnp.int32),
        compiler_params=pltpu.CompilerParams(vmem_limit_bytes=VMEM_LIMIT),
        name="route_threshold",
    )(aff_t)


def _select_kernel(a_ref, thr_ref, upper_ref, slot_t_ref, slot_ref, gate_ref, cnt_ref, seen_ref):
    @pl.when(pl.program_id(0) == 0)
    def _():
        seen_ref[...] = jnp.zeros_like(seen_ref)

    a = a_ref[...]
    bits = pltpu.bitcast(a, jnp.int32)
    thr = thr_ref[0][:, 0:1]
    need = thr_ref[1][:, 0:1].astype(F32)
    gt, eq = bits > thr, bits == thr
    marks = jnp.concatenate([jnp.where(gt, 1.0, 0.0), jnp.where(eq, 1.0, 0.0)], axis=0).astype(BF16)
    before = jnp.dot(marks, upper_ref[...], preferred_element_type=F32)
    gt_before, eq_before = before[:N_EXPERTS], before[N_EXPERTS:]
    seen = seen_ref[:, 0:1]
    eq_rank = seen + eq_before
    sel = jnp.logical_or(gt, jnp.logical_and(eq, eq_rank < need))
    rank = gt_before + jnp.minimum(eq_rank, need) - jnp.minimum(seen, need)
    slot_t = jnp.where(sel, rank, -1.0)
    slot_t_ref[...] = slot_t.astype(jnp.int32)
    cnt_ref[0] = jnp.broadcast_to(jnp.sum(jnp.where(sel, 1, 0), axis=1, keepdims=True), (N_EXPERTS, LANES))
    seen_ref[...] = seen_ref[...] + jnp.sum(jnp.where(eq, 1.0, 0.0), axis=1, keepdims=True)
    pad = jnp.zeros((LANES - N_EXPERTS, TT), F32)
    slot_ref[...] = jnp.concatenate([slot_t, pad], axis=0).T[:, :N_EXPERTS].astype(jnp.int32)
    gate_ref[...] = jnp.concatenate([a, pad], axis=0).T[:, :N_EXPERTS]


def _select(aff_t, thr):
    n_tok = aff_t.shape[1]
    n_tiles = n_tok // TT
    t = np.arange(TT)
    upper = jnp.asarray(t[:, None] < t[None, :], BF16)
    return pl.pallas_call(
        _select_kernel,
        grid=(n_tiles,),
        in_specs=[pl.BlockSpec((N_EXPERTS, TT), lambda i: (0, i)),
                  pl.BlockSpec((2, N_EXPERTS, LANES), lambda i: (0, 0, 0)),
                  pl.BlockSpec((TT, TT), lambda i: (0, 0))],
        out_specs=[pl.BlockSpec((N_EXPERTS, TT), lambda i: (0, i)),
                   pl.BlockSpec((TT, N_EXPERTS), lambda i: (i, 0)),
                   pl.BlockSpec((TT, N_EXPERTS), lambda i: (i, 0)),
                   pl.BlockSpec((1, N_EXPERTS, LANES), lambda i: (i, 0, 0))],
        out_shape=[jax.ShapeDtypeStruct((N_EXPERTS, n_tok), jnp.int32),
                   jax.ShapeDtypeStruct((n_tok, N_EXPERTS), jnp.int32),
                   jax.ShapeDtypeStruct((n_tok, N_EXPERTS), F32),
                   jax.ShapeDtypeStruct((n_tiles, N_EXPERTS, LANES), jnp.int32)],
        scratch_shapes=[pltpu.VMEM((N_EXPERTS, LANES), F32)],
        compiler_params=pltpu.CompilerParams(dimension_semantics=("arbitrary",)),
        name="route_select",
    )(aff_t, thr, upper)


def _dispatch_kernel(base_ref, cnt_ref, slot_ref, h_ref, xe_in_ref, xe_ref, stage_ref, extra_ref, sem, extra_sem):
    del xe_in_ref
    i = pl.program_id(0)
    par = i % 2
    row = lax.broadcasted_iota(jnp.int32, (WIN, TT), 0)
    slot = slot_ref[...]
    onehot = jnp.concatenate([jnp.where(slot[e:e + 1, :] == row, 1.0, 0.0).astype(BF16)
                              for e in range(N_EXPERTS)], axis=0)
    stage_ref[par] = jnp.dot(onehot, h_ref[...], preferred_element_type=F32).astype(BF16)

    def window_copy(e, start):
        return pltpu.make_async_copy(stage_ref.at[par, pl.ds(e * WIN, WIN), :],
                                     xe_ref.at[e, pl.ds(pl.multiple_of(start, SEG_ALIGN), WIN), :], sem.at[e])

    for e in range(N_EXPERTS):
        start = base_ref[i * N_EXPERTS + e]

        @pl.when(i > 0)
        def _():
            window_copy(e, start).wait()

        window_copy(e, start).start()

    def extra_windows(e, carry):
        start = base_ref[i * N_EXPERTS + e]

        def one(w, c):
            srow = slot_ref[pl.ds(e, 1), :] - (w + 1) * WIN
            oh = jnp.where(srow == row, 1.0, 0.0).astype(BF16)
            extra_ref[...] = jnp.dot(oh, h_ref[...], preferred_element_type=F32).astype(BF16)
            cp = pltpu.make_async_copy(
                extra_ref, xe_ref.at[e, pl.ds(pl.multiple_of(start + (w + 1) * WIN, SEG_ALIGN), WIN), :], extra_sem)
            cp.start()
            cp.wait()
            return c

        return lax.fori_loop(0, jnp.maximum(cnt_ref[i * N_EXPERTS + e] - 1, 0) // WIN, one, carry)

    lax.fori_loop(0, N_EXPERTS, extra_windows, 0)

    @pl.when(i == pl.num_programs(0) - 1)
    def _():
        for e in range(N_EXPERTS):
            window_copy(e, base_ref[i * N_EXPERTS + e]).wait()


def _dispatch(base, cnt, slot_t, h2, rows):
    n_tok, d = h2.shape
    xe0 = jnp.zeros((N_EXPERTS, rows, d), BF16)
    return pl.pallas_call(
        _dispatch_kernel,
        grid_spec=pltpu.PrefetchScalarGridSpec(
            num_scalar_prefetch=2,
            grid=(n_tok // TT,),
            in_specs=[pl.BlockSpec((N_EXPERTS, TT), lambda i, b, c: (0, i)),
                      pl.BlockSpec((TT, d), lambda i, b, c: (i, 0)),
                      pl.BlockSpec(memory_space=pl.ANY)],
            out_specs=pl.BlockSpec(memory_space=pl.ANY),
            scratch_shapes=[pltpu.VMEM((2, N_EXPERTS * WIN, d), BF16), pltpu.VMEM((WIN, d), BF16),
                            pltpu.SemaphoreType.DMA((N_EXPERTS,)), pltpu.SemaphoreType.DMA(())]),
        out_shape=jax.ShapeDtypeStruct((N_EXPERTS, rows, d), BF16),
        input_output_aliases={4: 0},
        compiler_params=pltpu.CompilerParams(dimension_semantics=("arbitrary",), vmem_limit_bytes=VMEM_LIMIT),
        name="route_dispatch",
    )(base, cnt, slot_t, h2, xe0)


def _combine_kernel(base_ref, cnt_ref, slot_ref, gate_ref, x1_ref, ga2_ref, g_ref, ye_ref, o_ref,
                    ybuf, extra_ref, acc_ref, sem, extra_sem):
    i = pl.program_id(0)
    par = i % 2

    def window_copy(step, e, buf):
        start = pl.multiple_of(base_ref[step * N_EXPERTS + e], SEG_ALIGN)
        return pltpu.make_async_copy(ye_ref.at[e, pl.ds(start, WIN), :],
                                     ybuf.at[buf, pl.ds(e * WIN, WIN), :], sem.at[buf])

    @pl.when(i == 0)
    def _():
        for e in range(N_EXPERTS):
            window_copy(0, e, 0).start()

    @pl.when(i + 1 < pl.num_programs(0))
    def _():
        for e in range(N_EXPERTS):
            window_copy(i + 1, e, 1 - par).start()

    slot, gate = slot_ref[...], gate_ref[...]
    lane = lax.broadcasted_iota(jnp.int32, (TT, WIN), 1)
    weights = jnp.concatenate([jnp.where(slot[:, e:e + 1] == lane, gate[:, e:e + 1], 0.0).astype(BF16)
                               for e in range(N_EXPERTS)], axis=1)
    for e in range(N_EXPERTS):
        window_copy(i, e, par).wait()
    acc_ref[...] = jnp.dot(weights, ybuf[par], preferred_element_type=F32)

    def extra_windows(e, carry):
        start = base_ref[i * N_EXPERTS + e]
        pick = lax.broadcasted_iota(jnp.int32, (TT, N_EXPERTS), 1) == e
        slot_e = jnp.sum(jnp.where(pick, slot, 0), axis=1, keepdims=True)
        gate_e = jnp.sum(jnp.where(pick, gate, 0.0), axis=1, keepdims=True)

        def one(w, c):
            cp = pltpu.make_async_copy(
                ye_ref.at[e, pl.ds(pl.multiple_of(start + (w + 1) * WIN, SEG_ALIGN), WIN), :], extra_ref, extra_sem)
            cp.start()
            wts = jnp.where(slot_e - (w + 1) * WIN == lane, gate_e, 0.0).astype(BF16)
            cp.wait()
            acc_ref[...] += jnp.dot(wts, extra_ref[...], preferred_element_type=F32)
            return c

        return lax.fori_loop(0, jnp.maximum(cnt_ref[i * N_EXPERTS + e] - 1, 0) // WIN, one, carry)

    lax.fori_loop(0, N_EXPERTS, extra_windows, 0)
    o_ref[...] = x1_ref[...] + ga2_ref[0] * (_rms(acc_ref[...]) * g_ref[...])


def _combine(base, cnt, slot, gate, x1, ga2, g_post2, ye, tiles_per_seq):
    n_tok, d = x1.shape
    tok = lambda w: pl.BlockSpec((TT, w), lambda i, b, c: (i, 0))
    return pl.pallas_call(
        _combine_kernel,
        grid_spec=pltpu.PrefetchScalarGridSpec(
            num_scalar_prefetch=2,
            grid=(n_tok // TT,),
            in_specs=[tok(N_EXPERTS), tok(N_EXPERTS), tok(d),
                      pl.BlockSpec((1, 1, d), lambda i, b, c: (i // tiles_per_seq, 0, 0)),
                      pl.BlockSpec((1, d), lambda i, b, c: (0, 0)),
                      pl.BlockSpec(memory_space=pl.ANY)],
            out_specs=tok(d),
            scratch_shapes=[pltpu.VMEM((2, N_EXPERTS * WIN, d), BF16), pltpu.VMEM((WIN, d), BF16),
                            pltpu.VMEM((TT, d), F32),
                            pltpu.SemaphoreType.DMA((2,)), pltpu.SemaphoreType.DMA(())]),
        out_shape=jax.ShapeDtypeStruct((n_tok, d), F32),
        compiler_params=pltpu.CompilerParams(dimension_semantics=("arbitrary",), vmem_limit_bytes=VMEM_LIMIT),
        name="route_combine_final",
    )(base, cnt, slot, gate, x1, ga2, g_post2.reshape(1, d), ye)


def _moe(x1, h2, aff_t, ga2, g_post2, wg, wu, wd, tm_expert=512):
    b, s, d = x1.shape
    n_tok = b * s
    n_tiles = n_tok // TT
    cap = CAPACITY_FACTOR * n_tok // N_EXPERTS
    thr = _thresholds(aff_t, cap)
    slot_t, slot, gate, cnt = _select(aff_t, thr)
    cnt = cnt[:, :, 0]
    padded = -(-cnt // SEG_ALIGN) * SEG_ALIGN
    base = jnp.cumsum(padded, axis=0) - padded
    total = jnp.sum(padded, axis=0)
    rows = cap + SEG_ALIGN * n_tiles + WIN
    rows = -(-rows // tm_expert) * tm_expert
    base, cnt = base.reshape(-1).astype(jnp.int32), cnt.reshape(-1).astype(jnp.int32)
    xe = _dispatch(base, cnt, slot_t, h2.reshape(n_tok, d), rows)
    ye = _experts(total.astype(jnp.int32), xe, wg, wu, wd, tm=tm_expert)
    out = _combine(base, cnt, slot, gate, x1.reshape(n_tok, d), ga2, g_post2, ye, s // TT)
    return out.reshape(b, s, d)


def _rope_tables(s):
    half = D_HEAD // 2
    inv = 1.0 / (ROPE_BASE ** (jnp.arange(half, dtype=F32) / half))
    ang = jnp.arange(s, dtype=F32)[:, None] * inv[None, :]
    cos, sin = jnp.cos(ang), jnp.sin(ang)
    return jnp.concatenate([cos, cos], axis=-1), jnp.concatenate([-sin, sin], axis=-1)


def _layer(x, mod, p):
    b, s, d = x.shape
    sh1, sc1, ga1, sh2, sc2, ga2 = [m[:, None, :] for m in jnp.split(mod, 6, axis=-1)]
    cos2, sin2 = _rope_tables(s)
    u = _inproj(x, sc1, sh1, p["g_pre1"], cos2, sin2, p["w_in"])
    y_ret = _retention(u, p["log_gamma"], p["gn_ret"])
    y_hg = _hgrn2(u, p["lb"], p["gn_hg"], p["hg_consts"])
    x1, h2, aff_t = _merge(x, y_ret, y_hg, u, ga1, sc2, sh2, p["g_post1"], p["g_pre2"],
                           p["w_br_ret"], p["w_br_hg"], p["w_out"], p["w_router_t"])
    return _moe(x1, h2, aff_t, ga2, p["g_post2"], p["w_gate"], p["w_up"], p["w_down"])


def kernel(x_prompt, x_sample, c_prompt, c_sample, w_ada, b_ada, g_pre1, g_post1, w_in, gn_ret, gn_hg, lb_logits, w_br_ret, w_br_hg, w_out, g_pre2, g_post2, w_router, w_gate, w_up, w_down):
    lower_bounds = jnp.cumsum(jax.nn.softmax(lb_logits.astype(F32), axis=0), axis=0)
    log_gamma = jnp.log(1.0 - jnp.power(2.0, -5.0 - jnp.arange(N_HEADS, dtype=F32)))
    consts = _hg_constants()
    n_gate = 2 * D_MODEL
    xp, xs = x_prompt, x_sample
    bp, bs = xp.shape[0], xs.shape[0]
    for l in range(w_ada.shape[0]):
        c_all = jnp.concatenate([c_prompt, c_sample, jnp.zeros((-(bp + bs) % 8, D_MODEL), F32)], axis=0)
        mod = _ada(c_all, w_ada[l], b_ada[l])
        w_in_l = w_in[l]
        p = dict(
            g_pre1=g_pre1[l], g_post1=g_post1[l], g_pre2=g_pre2[l], g_post2=g_post2[l],
            w_in=jnp.concatenate([w_in_l[:, -n_gate:], w_in_l[:, :-n_gate]], axis=1).astype(BF16),
            log_gamma=jnp.broadcast_to(log_gamma[:, None, None], (N_HEADS, 1, LANES)),
            gn_ret=gn_ret[l], gn_hg=gn_hg[l], lb=lower_bounds[l], hg_consts=consts,
            w_br_ret=w_br_ret[l].astype(BF16), w_br_hg=w_br_hg[l].astype(BF16), w_out=w_out[l].astype(BF16),
            w_router_t=w_router[l].T,
            w_gate=w_gate[l].astype(BF16), w_up=w_up[l].astype(BF16), w_down=w_down[l].astype(BF16),
        )
        xp = _layer(xp, mod[:bp], p)
        xs = _layer(xs, mod[bp:bp + bs], p)
    return (xp, xs)
```

```python
import functools

import numpy as np
import jax
import jax.numpy as jnp
from jax import lax
from jax.experimental import pallas as pl
from jax.experimental.pallas import tpu as pltpu

F32 = jnp.float32
BF16 = jnp.bfloat16

D_MODEL = 1024
N_HEADS = 4
D_HEAD = 128
N_EXPERTS = 16
CAPACITY_FACTOR = 2
D_EXPERT = 2048
ROPE_BASE = 10000.0
NORM_EPS = 1e-6
CHUNK = 128
LANES = 128
VMEM_LIMIT = 56 * 1024 * 1024

D_IN = 2 * D_MODEL + 9 * N_HEADS * D_HEAD
COL_RETQ, COL_RETK, COL_RETV, COL_RETG = 16, 20, 24, 28
COL_HGQ, COL_HGI, COL_HGFF, COL_HGFB, COL_HGG = 32, 36, 40, 44, 48
TN_IN = 512
ROPE_CHUNKS = (COL_RETQ * LANES // TN_IN, COL_RETK * LANES // TN_IN)

LOG2_E = 1.4426950408889634
NT_DIMS = (((1,), (1,)), ((), ()))
TN_DIMS = (((0,), (0,)), ((), ()))


def _sigmoid(x):
    return 1.0 / (1.0 + jnp.exp(-x))


def _rms(x, eps=NORM_EPS):
    return x * lax.rsqrt(jnp.mean(x * x, axis=-1, keepdims=True) + eps)


def _ada_kernel(c_ref, w_ref, b_ref, o_ref):
    c = c_ref[...]
    o_ref[...] = jnp.dot(c * _sigmoid(c), w_ref[...], preferred_element_type=F32,
                         precision=lax.Precision.HIGHEST) + b_ref[...]


def _ada(c_all, w_ada, b_ada):
    rows = c_all.shape[0]
    n_out = w_ada.shape[1]
    tn = 1024
    return pl.pallas_call(
        _ada_kernel,
        grid=(n_out // tn,),
        in_specs=[pl.BlockSpec((rows, D_MODEL), lambda n: (0, 0)),
                  pl.BlockSpec((D_MODEL, tn), lambda n: (0, n)),
                  pl.BlockSpec((1, tn), lambda n: (0, n))],
        out_specs=pl.BlockSpec((rows, tn), lambda n: (0, n)),
        out_shape=jax.ShapeDtypeStruct((rows, n_out), F32),
        name="ada_mod",
    )(c_all, w_ada, b_ada.reshape(1, n_out))


def _inproj_kernel(x_ref, sc_ref, sh_ref, g_ref, cos_ref, sin_ref, w_ref, u_ref):
    h = (_rms(x_ref[0]) * g_ref[...] * (1.0 + sc_ref[0]) + sh_ref[0]).astype(BF16)
    for n in range(D_IN // TN_IN):
        acc = jnp.dot(h, w_ref[:, n * TN_IN:(n + 1) * TN_IN], preferred_element_type=F32)
        if n in ROPE_CHUNKS:
            scale = 1.0 if n == ROPE_CHUNKS[0] else D_HEAD ** -0.5
            for hd in range(TN_IN // D_HEAD):
                blk = acc[:, hd * D_HEAD:(hd + 1) * D_HEAD]
                rot = blk * cos_ref[...] + pltpu.roll(blk, D_HEAD // 2, axis=1) * sin_ref[...]
                lo = n * TN_IN + hd * D_HEAD
                u_ref[0, :, lo:lo + D_HEAD] = (rot * scale).astype(BF16)
        else:
            u_ref[0, :, n * TN_IN:(n + 1) * TN_IN] = acc.astype(BF16)


def _inproj(x, sc, sh, g, cos2, sin2, w_bf16, tm=512):
    b, s, d = x.shape
    return pl.pallas_call(
        _inproj_kernel,
        grid=(b, s // tm),
        in_specs=[pl.BlockSpec((1, tm, d), lambda i, j: (i, j, 0)),
                  pl.BlockSpec((1, 1, d), lambda i, j: (i, 0, 0)),
                  pl.BlockSpec((1, 1, d), lambda i, j: (i, 0, 0)),
                  pl.BlockSpec((1, d), lambda i, j: (0, 0)),
                  pl.BlockSpec((tm, D_HEAD), lambda i, j: (j, 0)),
                  pl.BlockSpec((tm, D_HEAD), lambda i, j: (j, 0)),
                  pl.BlockSpec((d, D_IN), lambda i, j: (0, 0))],
        out_specs=pl.BlockSpec((1, tm, D_IN), lambda i, j: (i, j, 0)),
        out_shape=jax.ShapeDtypeStruct((b, s, D_IN), BF16),
        compiler_params=pltpu.CompilerParams(dimension_semantics=("parallel", "parallel"),
                                             vmem_limit_bytes=VMEM_LIMIT),
        name="inproj",
    )(x, sc, sh, g.reshape(1, d), cos2, sin2, w_bf16)


def _ret_kernel(q_ref, k_ref, v_ref, g_ref, lg_ref, gn_ref, o_ref, yf_ref, yb_ref, *, n_chunks):
    c = CHUNK
    lg = lg_ref[0]
    pos = lax.broadcasted_iota(jnp.int32, (c, LANES), 0).astype(F32)
    dist = jnp.abs(lax.broadcasted_iota(jnp.int32, (c, c), 0)
                   - lax.broadcasted_iota(jnp.int32, (c, c), 1)).astype(F32)
    decay = jnp.exp(lg * dist)
    q_fwd = jnp.exp(lg * (pos + 1.0))
    k_fwd = jnp.exp(lg * (c - 1.0 - pos))
    q_bwd = jnp.exp(lg * (c - pos))
    k_bwd = jnp.exp(lg * pos)
    g_chunk = jnp.exp(lg * c)

    def scaled(t, f):
        return (t.astype(F32) * f).astype(BF16)

    def sweep(n, carry):
        r_f, r_b = carry
        sl = pl.ds(pl.multiple_of(n * c, c), c)
        q, k, v = q_ref[0, sl, :], k_ref[0, sl, :], v_ref[0, sl, :]
        s = lax.dot_general(q, k, NT_DIMS, preferred_element_type=F32) * decay
        lhs = jnp.concatenate([s.astype(BF16), scaled(q, q_fwd)], axis=1)
        rhs = jnp.concatenate([v, r_f.astype(BF16)], axis=0)
        yf_ref[sl, :] = jnp.dot(lhs, rhs, preferred_element_type=F32)
        r_f = g_chunk * r_f + lax.dot_general(scaled(k, k_fwd), v, TN_DIMS, preferred_element_type=F32)
        sl = pl.ds(pl.multiple_of((n_chunks - 1 - n) * c, c), c)
        q, k, v = q_ref[0, sl, :], k_ref[0, sl, :], v_ref[0, sl, :]
        yb_ref[sl, :] = jnp.dot(scaled(q, q_bwd), r_b.astype(BF16), preferred_element_type=F32)
        r_b = g_chunk * r_b + lax.dot_general(scaled(k, k_bwd), v, TN_DIMS, preferred_element_type=F32)
        return r_f, r_b

    zero = jnp.zeros((D_HEAD, D_HEAD), F32)
    lax.fori_loop(0, n_chunks, sweep, (zero, zero), unroll=2)

    def finish(n, carry):
        sl = pl.ds(pl.multiple_of(n * c, c), c)
        y = yf_ref[sl, :] + yb_ref[sl, :]
        yc = y - jnp.mean(y, axis=-1, keepdims=True)
        g = g_ref[0, sl, :].astype(F32)
        o_ref[0, sl, :] = (_rms(yc) * gn_ref[0] * (g * _sigmoid(g))).astype(BF16)
        return carry

    lax.fori_loop(0, n_chunks, finish, 0, unroll=4)


def _retention(u, log_gamma, gn):
    b, s, _ = u.shape

    def col(c0):
        return pl.BlockSpec((1, s, D_HEAD), lambda i, h: (i, 0, c0 + h))

    per_head = pl.BlockSpec((1, 1, D_HEAD), lambda i, h: (h, 0, 0))
    return pl.pallas_call(
        functools.partial(_ret_kernel, n_chunks=s // CHUNK),
        grid=(b, N_HEADS),
        in_specs=[col(COL_RETQ), col(COL_RETK), col(COL_RETV), col(COL_RETG), per_head, per_head],
        out_specs=pl.BlockSpec((1, s, D_HEAD), lambda i, h: (i, 0, h)),
        out_shape=jax.ShapeDtypeStruct((b, s, N_HEADS * D_HEAD), BF16),
        scratch_shapes=[pltpu.VMEM((s, D_HEAD), F32), pltpu.VMEM((s, D_HEAD), F32)],
        compiler_params=pltpu.CompilerParams(dimension_semantics=("parallel", "parallel"),
                                             vmem_limit_bytes=VMEM_LIMIT),
        name="retention",
    )(u, u, u, u, log_gamma, gn.reshape(N_HEADS, 1, D_HEAD))


N_LEVELS = 8
N_MXU_LEVELS = 4


def _hg_constants():
    c = CHUNK
    p = np.arange(c)
    tril = (p[None, :] <= p[:, None]).astype(np.float64)
    mats, masks = [tril], [np.eye(c)]
    for l in range(1, N_LEVELS):
        s, half = 2 ** l, 2 ** (l - 1)
        blk = p // s
        sel = np.zeros((c, c))
        sel[p, blk * s + half - 1] = 1.0
        mats.append(tril - sel @ tril)
        upper = (p % s) >= half
        masks.append(((blk[:, None] == blk[None, :]) & upper[:, None] & (~upper)[None, :]).astype(np.float64))
    mats = mats[:N_MXU_LEVELS]
    mat_f = np.concatenate(mats, axis=0)
    mat_b = np.concatenate([m[::-1, ::-1] for m in mats], axis=0)
    mask_f = np.stack(masks)
    mask_b = np.stack([m.T for m in masks])
    tile2 = lambda m: np.concatenate([m, m], axis=1)
    return (jnp.asarray(tile2(mat_f), BF16), jnp.asarray(tile2(mat_b), BF16),
            jnp.asarray(mask_f, F32), jnp.asarray(mask_b, F32))


def _block_offsets(cum, level, forward):
    s = 2 ** level
    ref_row = s // 2 - 1 if forward else s // 2
    blocks = cum.reshape(CHUNK // s, s, LANES)
    return (blocks - blocks[:, ref_row:ref_row + 1, :]).reshape(CHUNK, LANES)


def _hg_chunk(q, v, z, lb, state, mat_ref, mask_ref, *, forward):
    c = CHUNK
    sg = _sigmoid(z)
    lf = jnp.log(lb + (1.0 - lb) * sg) * LOG2_E
    kk = (1.0 - lb) * (1.0 - sg)
    hi = lf.astype(BF16)
    mid = (lf - hi.astype(F32)).astype(BF16)
    offs = jnp.dot(mat_ref[...], jnp.concatenate([hi, mid], axis=0), preferred_element_type=F32)
    cum = offs[0:c]
    qf = q.astype(F32)
    a = lax.dot_general(q, kk.astype(BF16), NT_DIMS, preferred_element_type=F32) * mask_ref[0]
    for l in range(1, N_LEVELS):
        off = offs[l * c:(l + 1) * c] if l < N_MXU_LEVELS else _block_offsets(cum, l, forward)
        e = jnp.exp2(-jnp.abs(off))
        a += lax.dot_general((qf * e).astype(BF16), (kk * e).astype(BF16), NT_DIMS,
                             preferred_element_type=F32) * mask_ref[l]
    y = jnp.dot(a.astype(BF16), v, preferred_element_type=F32)
    y += lax.dot_general((qf * jnp.exp2(cum)).astype(BF16), state.astype(BF16), NT_DIMS,
                         preferred_element_type=F32)
    total = cum[c - 1:c] if forward else cum[0:1]
    kd = (kk * jnp.exp2(total - cum)).astype(BF16)
    new_state = state * jnp.exp2(total) + lax.dot_general(v, kd, TN_DIMS, preferred_element_type=F32)
    return y, new_state


def _hg_kernel(q_ref, i_ref, zf_ref, zb_ref, g_ref, lb_ref, gn_ref, matf_ref, matb_ref, maskf_ref, maskb_ref,
               o_ref, yf_ref, yb_ref, *, n_chunks):
    c = CHUNK
    lb = lb_ref[0]

    def sweep(n, carry):
        s_f, s_b = carry
        sl = pl.ds(pl.multiple_of(n * c, c), c)
        yf_ref[sl, :], s_f = _hg_chunk(q_ref[0, sl, :], i_ref[0, sl, :], zf_ref[0, sl, :].astype(F32), lb, s_f,
                                       matf_ref, maskf_ref, forward=True)
        sl = pl.ds(pl.multiple_of((n_chunks - 1 - n) * c, c), c)
        yb_ref[sl, :], s_b = _hg_chunk(q_ref[0, sl, :], i_ref[0, sl, :], zb_ref[0, sl, :].astype(F32), lb, s_b,
                                       matb_ref, maskb_ref, forward=False)
        return s_f, s_b

    zero = jnp.zeros((D_HEAD, D_HEAD), F32)
    lax.fori_loop(0, n_chunks, sweep, (zero, zero), unroll=2)

    def finish(n, carry):
        sl = pl.ds(pl.multiple_of(n * c, c), c)
        g = g_ref[0, sl, :].astype(F32)
        o_ref[0, sl, :] = (_rms(yf_ref[sl, :] + yb_ref[sl, :]) * gn_ref[0] * (g * _sigmoid(g))).astype(BF16)
        return carry

    lax.fori_loop(0, n_chunks, finish, 0, unroll=4)


def _hgrn2(u, lb, gn, consts):
    b, s, _ = u.shape
    mat_f, mat_b, mask_f, mask_b = consts

    def col(c0):
        return pl.BlockSpec((1, s, D_HEAD), lambda i, h: (i, 0, c0 + h))

    def whole(a):
        return pl.BlockSpec(a.shape, lambda i, h: (0,) * a.ndim)

    per_head = pl.BlockSpec((1, 1, D_HEAD), lambda i, h: (h, 0, 0))
    return pl.pallas_call(
        functools.partial(_hg_kernel, n_chunks=s // CHUNK),
        grid=(b, N_HEADS),
        in_specs=[col(COL_HGQ), col(COL_HGI), col(COL_HGFF), col(COL_HGFB), col(COL_HGG), per_head, per_head,
                  whole(mat_f), whole(mat_b), whole(mask_f), whole(mask_b)],
        out_specs=pl.BlockSpec((1, s, D_HEAD), lambda i, h: (i, 0, h)),
        out_shape=jax.ShapeDtypeStruct((b, s, N_HEADS * D_HEAD), BF16),
        scratch_shapes=[pltpu.VMEM((s, D_HEAD), F32), pltpu.VMEM((s, D_HEAD), F32)],
        compiler_params=pltpu.CompilerParams(dimension_semantics=("parallel", "parallel"),
                                             vmem_limit_bytes=VMEM_LIMIT),
        name="hgrn2",
    )(u, u, u, u, u, lb.reshape(N_HEADS, 1, D_HEAD), gn.reshape(N_HEADS, 1, D_HEAD),
      mat_f, mat_b, mask_f, mask_b)


def _merge_kernel(x_ref, yr_ref, yh_ref, za_ref, zb_ref, ga1_ref, sc2_ref, sh2_ref, gpost1_ref, gpre2_ref,
                  wr_ref, wh_ref, wo_ref, wrt_ref, x1_ref, h2_ref, aff_ref):
    br = jnp.dot(yr_ref[0], wr_ref[...], preferred_element_type=F32)
    bh = jnp.dot(yh_ref[0], wh_ref[...], preferred_element_type=F32)
    merged = _sigmoid(za_ref[0].astype(F32)) * br + _sigmoid(zb_ref[0].astype(F32)) * bh
    o = jnp.dot(merged.astype(BF16), wo_ref[...], preferred_element_type=F32)
    x1 = x_ref[0] + ga1_ref[0] * (_rms(o) * gpost1_ref[...])
    x1_ref[0] = x1
    h2 = _rms(x1) * gpre2_ref[...] * (1.0 + sc2_ref[0]) + sh2_ref[0]
    h2_ref[0] = h2.astype(BF16)
    logits = lax.dot_general(wrt_ref[...], h2, NT_DIMS, preferred_element_type=F32,
                             precision=lax.Precision.HIGHEST)
    ex = jnp.exp(logits - jnp.max(logits, axis=0, keepdims=True))
    aff_ref[...] = ex / jnp.sum(ex, axis=0, keepdims=True)


def _merge(x, y_ret, y_hg, u, ga1, sc2, sh2, g_post1, g_pre2, w_br_ret, w_br_hg, w_out, w_router_t, tm=512):
    b, s, d = x.shape
    nj = s // tm
    tok = lambda w: pl.BlockSpec((1, tm, w), lambda i, j: (i, j, 0))
    per_b = pl.BlockSpec((1, 1, d), lambda i, j: (i, 0, 0))
    vec = pl.BlockSpec((1, d), lambda i, j: (0, 0))
    whole = lambda a: pl.BlockSpec(a.shape, lambda i, j: (0,) * a.ndim)
    return pl.pallas_call(
        _merge_kernel,
        grid=(b, nj),
        in_specs=[tok(d), tok(N_HEADS * D_HEAD), tok(N_HEADS * D_HEAD),
                  pl.BlockSpec((1, tm, d), lambda i, j: (i, j, 0)),
                  pl.BlockSpec((1, tm, d), lambda i, j: (i, j, 1)),
                  per_b, per_b, per_b, vec, vec,
                  whole(w_br_ret), whole(w_br_hg), whole(w_out), whole(w_router_t)],
        out_specs=[tok(d), tok(d), pl.BlockSpec((N_EXPERTS, tm), lambda i, j: (0, i * nj + j))],
        out_shape=[jax.ShapeDtypeStruct((b, s, d), F32), jax.ShapeDtypeStruct((b, s, d), BF16),
                   jax.ShapeDtypeStruct((N_EXPERTS, b * s), F32)],
        compiler_params=pltpu.CompilerParams(dimension_semantics=("parallel", "parallel"),
                                             vmem_limit_bytes=VMEM_LIMIT),
        name="merge_outproj_router",
    )(x, y_ret, y_hg, u, u, ga1, sc2, sh2, g_post1.reshape(1, d), g_pre2.reshape(1, d),
      w_br_ret, w_br_hg, w_out, w_router_t)


TF_EXPERT = 512


def _expert_kernel(total_ref, x_ref, wg_ref, wu_ref, wd_ref, o_ref, *, tm):
    live = pl.program_id(1) * tm < total_ref[pl.program_id(0)]

    @pl.when(live)
    def _():
        x = x_ref[0]
        acc = jnp.zeros((tm, D_MODEL), F32)
        for f in range(D_EXPERT // TF_EXPERT):
            sl = slice(f * TF_EXPERT, (f + 1) * TF_EXPERT)
            a = jnp.dot(x, wg_ref[0, :, sl], preferred_element_type=F32)
            up = jnp.dot(x, wu_ref[0, :, sl], preferred_element_type=F32)
            acc += jnp.dot((a * _sigmoid(a) * up).astype(BF16), wd_ref[0, sl, :], preferred_element_type=F32)
        o_ref[0] = acc.astype(BF16)

    @pl.when(jnp.logical_not(live))
    def _():
        o_ref[0] = jnp.zeros((tm, D_MODEL), BF16)


def _experts(total, xe, wg, wu, wd, tm=512):
    e, rows, d = xe.shape
    return pl.pallas_call(
        functools.partial(_expert_kernel, tm=tm),
        grid_spec=pltpu.PrefetchScalarGridSpec(
            num_scalar_prefetch=1,
            grid=(e, rows // tm),
            in_specs=[pl.BlockSpec((1, tm, d), lambda i, j, t: (i, j, 0)),
                      pl.BlockSpec((1, d, D_EXPERT), lambda i, j, t: (i, 0, 0)),
                      pl.BlockSpec((1, d, D_EXPERT), lambda i, j, t: (i, 0, 0)),
                      pl.BlockSpec((1, D_EXPERT, d), lambda i, j, t: (i, 0, 0))],
            out_specs=pl.BlockSpec((1, tm, d), lambda i, j, t: (i, j, 0))),
        out_shape=jax.ShapeDtypeStruct((e, rows, d), BF16),
        compiler_params=pltpu.CompilerParams(dimension_semantics=("parallel", "parallel"),
                                             vmem_limit_bytes=VMEM_LIMIT),
        name="expert_ffn",
    )(total, xe, wg, wu, wd)


TT = 512
WIN = 128
SEG_ALIGN = 16


MIN_NORMAL = 2.0 ** -126
MANTISSA_STEPS = 53


def _threshold_kernel(a_ref, o_ref, *, cap):
    def count_ge(x):
        return jnp.sum(jnp.where(a_ref[...] >= x, 1.0, 0.0), axis=1, keepdims=True)

    lo = jnp.where(count_ge(MIN_NORMAL) >= cap, MIN_NORMAL, 0.0) * jnp.ones((N_EXPERTS, 1), F32)
    for j in (64, 32, 16, 8, 4, 2, 1):
        cand = lo * (2.0 ** j)
        lo = jnp.where(count_ge(cand) >= cap, cand, lo)
    hi = jnp.where(lo > 0.0, 2.0 * lo, MIN_NORMAL)

    def halve(_, bounds):
        lo, hi = bounds
        mid = 0.5 * (lo + hi)
        ok = count_ge(mid) >= cap
        return jnp.where(ok, mid, lo), jnp.where(ok, hi, mid)

    lo, hi = lax.fori_loop(0, MANTISSA_STEPS, halve, (lo, hi))
    a = a_ref[...]
    thr = jnp.min(jnp.where(a >= lo, a, 2.0), axis=1, keepdims=True)
    n_gt = jnp.sum(jnp.where(a > thr, 1.0, 0.0), axis=1, keepdims=True)
    o_ref[0] = jnp.broadcast_to(thr, (N_EXPERTS, LANES))
    o_ref[1] = jnp.broadcast_to(cap - n_gt, (N_EXPERTS, LANES))


def _thresholds(aff_t, cap):
    return pl.pallas_call(
        functools.partial(_threshold_kernel, cap=cap),
        out_shape=jax.ShapeDtypeStruct((2, N_EXPERTS, LANES), F32),
        compiler_params=pltpu.CompilerParams(vmem_limit_bytes=VMEM_LIMIT),
        name="route_threshold",
    )(aff_t)


def _select_kernel(a_ref, thr_ref, upper_ref, slot_t_ref, slot_ref, gate_ref, cnt_ref, seen_ref):
    @pl.when(pl.program_id(0) == 0)
    def _():
        seen_ref[...] = jnp.zeros_like(seen_ref)

    a = a_ref[...]
    thr = thr_ref[0][:, 0:1]
    need = thr_ref[1][:, 0:1]
    gt, eq = a > thr, a == thr
    marks = jnp.concatenate([jnp.where(gt, 1.0, 0.0), jnp.where(eq, 1.0, 0.0)], axis=0).astype(BF16)
    before = jnp.dot(marks, upper_ref[...], preferred_element_type=F32)
    gt_before, eq_before = before[:N_EXPERTS], before[N_EXPERTS:]
    seen = seen_ref[:, 0:1]
    eq_rank = seen + eq_before
    sel = jnp.logical_or(gt, jnp.logical_and(eq, eq_rank < need))
    rank = gt_before + jnp.minimum(eq_rank, need) - jnp.minimum(seen, need)
    slot_t = jnp.where(sel, rank, -1.0)
    slot_t_ref[...] = slot_t.astype(jnp.int32)
    cnt_ref[0] = jnp.broadcast_to(jnp.sum(jnp.where(sel, 1, 0), axis=1, keepdims=True), (N_EXPERTS, LANES))
    seen_ref[...] = seen_ref[...] + jnp.sum(jnp.where(eq, 1.0, 0.0), axis=1, keepdims=True)
    pad = jnp.zeros((LANES - N_EXPERTS, TT), F32)
    slot_ref[...] = jnp.concatenate([slot_t, pad], axis=0).T[:, :N_EXPERTS].astype(jnp.int32)
    gate_ref[...] = jnp.concatenate([a, pad], axis=0).T[:, :N_EXPERTS]


def _select(aff_t, thr):
    n_tok = aff_t.shape[1]
    n_tiles = n_tok // TT
    t = np.arange(TT)
    upper = jnp.asarray(t[:, None] < t[None, :], BF16)
    return pl.pallas_call(
        _select_kernel,
        grid=(n_tiles,),
        in_specs=[pl.BlockSpec((N_EXPERTS, TT), lambda i: (0, i)),
                  pl.BlockSpec((2, N_EXPERTS, LANES), lambda i: (0, 0, 0)),
                  pl.BlockSpec((TT, TT), lambda i: (0, 0))],
        out_specs=[pl.BlockSpec((N_EXPERTS, TT), lambda i: (0, i)),
                   pl.BlockSpec((TT, N_EXPERTS), lambda i: (i, 0)),
                   pl.BlockSpec((TT, N_EXPERTS), lambda i: (i, 0)),
                   pl.BlockSpec((1, N_EXPERTS, LANES), lambda i: (i, 0, 0))],
        out_shape=[jax.ShapeDtypeStruct((N_EXPERTS, n_tok), jnp.int32),
                   jax.ShapeDtypeStruct((n_tok, N_EXPERTS), jnp.int32),
                   jax.ShapeDtypeStruct((n_tok, N_EXPERTS), F32),
                   jax.ShapeDtypeStruct((n_tiles, N_EXPERTS, LANES), jnp.int32)],
        scratch_shapes=[pltpu.VMEM((N_EXPERTS, LANES), F32)],
        compiler_params=pltpu.CompilerParams(dimension_semantics=("arbitrary",)),
        name="route_select",
    )(aff_t, thr, upper)


def _dispatch_kernel(base_ref, cnt_ref, slot_ref, h_ref, xe_in_ref, xe_ref, stage_ref, extra_ref, sem, extra_sem):
    del xe_in_ref
    i = pl.program_id(0)
    par = i % 2
    row = lax.broadcasted_iota(jnp.int32, (WIN, TT), 0)
    slot = slot_ref[...]
    onehot = jnp.concatenate([jnp.where(slot[e:e + 1, :] == row, 1.0, 0.0).astype(BF16)
                              for e in range(N_EXPERTS)], axis=0)
    stage_ref[par] = jnp.dot(onehot, h_ref[...], preferred_element_type=F32).astype(BF16)

    def window_copy(e, start):
        return pltpu.make_async_copy(stage_ref.at[par, pl.ds(e * WIN, WIN), :],
                                     xe_ref.at[e, pl.ds(pl.multiple_of(start, SEG_ALIGN), WIN), :], sem.at[e])

    for e in range(N_EXPERTS):
        start = base_ref[i * N_EXPERTS + e]

        @pl.when(i > 0)
        def _():
            window_copy(e, start).wait()

        window_copy(e, start).start()

    def extra_windows(e, carry):
        start = base_ref[i * N_EXPERTS + e]

        def one(w, c):
            srow = slot_ref[pl.ds(e, 1), :] - (w + 1) * WIN
            oh = jnp.where(srow == row, 1.0, 0.0).astype(BF16)
            extra_ref[...] = jnp.dot(oh, h_ref[...], preferred_element_type=F32).astype(BF16)
            cp = pltpu.make_async_copy(
                extra_ref, xe_ref.at[e, pl.ds(pl.multiple_of(start + (w + 1) * WIN, SEG_ALIGN), WIN), :], extra_sem)
            cp.start()
            cp.wait()
            return c

        return lax.fori_loop(0, jnp.maximum(cnt_ref[i * N_EXPERTS + e] - 1, 0) // WIN, one, carry)

    lax.fori_loop(0, N_EXPERTS, extra_windows, 0)

    @pl.when(i == pl.num_programs(0) - 1)
    def _():
        for e in range(N_EXPERTS):
            window_copy(e, base_ref[i * N_EXPERTS + e]).wait()


def _dispatch(base, cnt, slot_t, h2, rows):
    n_tok, d = h2.shape
    xe0 = jnp.zeros((N_EXPERTS, rows, d), BF16)
    return pl.pallas_call(
        _dispatch_kernel,
        grid_spec=pltpu.PrefetchScalarGridSpec(
            num_scalar_prefetch=2,
            grid=(n_tok // TT,),
            in_specs=[pl.BlockSpec((N_EXPERTS, TT), lambda i, b, c: (0, i)),
                      pl.BlockSpec((TT, d), lambda i, b, c: (i, 0)),
                      pl.BlockSpec(memory_space=pl.ANY)],
            out_specs=pl.BlockSpec(memory_space=pl.ANY),
            scratch_shapes=[pltpu.VMEM((2, N_EXPERTS * WIN, d), BF16), pltpu.VMEM((WIN, d), BF16),
                            pltpu.SemaphoreType.DMA((N_EXPERTS,)), pltpu.SemaphoreType.DMA(())]),
        out_shape=jax.ShapeDtypeStruct((N_EXPERTS, rows, d), BF16),
        input_output_aliases={4: 0},
        compiler_params=pltpu.CompilerParams(dimension_semantics=("arbitrary",), vmem_limit_bytes=VMEM_LIMIT),
        name="route_dispatch",
    )(base, cnt, slot_t, h2, xe0)


def _combine_kernel(base_ref, cnt_ref, slot_ref, gate_ref, x1_ref, ga2_ref, g_ref, ye_ref, o_ref,
                    ybuf, extra_ref, acc_ref, sem, extra_sem):
    i = pl.program_id(0)
    par = i % 2

    def window_copy(step, e, buf):
        start = pl.multiple_of(base_ref[step * N_EXPERTS + e], SEG_ALIGN)
        return pltpu.make_async_copy(ye_ref.at[e, pl.ds(start, WIN), :],
                                     ybuf.at[buf, pl.ds(e * WIN, WIN), :], sem.at[buf])

    @pl.when(i == 0)
    def _():
        for e in range(N_EXPERTS):
            window_copy(0, e, 0).start()

    @pl.when(i + 1 < pl.num_programs(0))
    def _():
        for e in range(N_EXPERTS):
            window_copy(i + 1, e, 1 - par).start()

    slot, gate = slot_ref[...], gate_ref[...]
    lane = lax.broadcasted_iota(jnp.int32, (TT, WIN), 1)
    weights = jnp.concatenate([jnp.where(slot[:, e:e + 1] == lane, gate[:, e:e + 1], 0.0).astype(BF16)
                               for e in range(N_EXPERTS)], axis=1)
    for e in range(N_EXPERTS):
        window_copy(i, e, par).wait()
    acc_ref[...] = jnp.dot(weights, ybuf[par], preferred_element_type=F32)

    def extra_windows(e, carry):
        start = base_ref[i * N_EXPERTS + e]

        def one(w, c):
            cp = pltpu.make_async_copy(
                ye_ref.at[e, pl.ds(pl.multiple_of(start + (w + 1) * WIN, SEG_ALIGN), WIN), :], extra_ref, extra_sem)
            cp.start()
            pick = lax.broadcasted_iota(jnp.int32, (TT, N_EXPERTS), 1) == e
            slot_e = jnp.sum(jnp.where(pick, slot_ref[...], 0), axis=1, keepdims=True)
            gate_e = jnp.sum(jnp.where(pick, gate_ref[...], 0.0), axis=1, keepdims=True)
            wts = jnp.where(slot_e - (w + 1) * WIN == lane, gate_e, 0.0).astype(BF16)
            cp.wait()
            acc_ref[...] += jnp.dot(wts, extra_ref[...], preferred_element_type=F32)
            return c

        return lax.fori_loop(0, jnp.maximum(cnt_ref[i * N_EXPERTS + e] - 1, 0) // WIN, one, carry)

    lax.fori_loop(0, N_EXPERTS, extra_windows, 0)
    o_ref[...] = x1_ref[...] + ga2_ref[0] * (_rms(acc_ref[...]) * g_ref[...])


def _combine(base, cnt, slot, gate, x1, ga2, g_post2, ye, tiles_per_seq):
    n_tok, d = x1.shape
    tok = lambda w: pl.BlockSpec((TT, w), lambda i, b, c: (i, 0))
    return pl.pallas_call(
        _combine_kernel,
        grid_spec=pltpu.PrefetchScalarGridSpec(
            num_scalar_prefetch=2,
            grid=(n_tok // TT,),
            in_specs=[tok(N_EXPERTS), tok(N_EXPERTS), tok(d),
                      pl.BlockSpec((1, 1, d), lambda i, b, c: (i // tiles_per_seq, 0, 0)),
                      pl.BlockSpec((1, d), lambda i, b, c: (0, 0)),
                      pl.BlockSpec(memory_space=pl.ANY)],
            out_specs=tok(d),
            scratch_shapes=[pltpu.VMEM((2, N_EXPERTS * WIN, d), BF16), pltpu.VMEM((WIN, d), BF16),
                            pltpu.VMEM((TT, d), F32),
                            pltpu.SemaphoreType.DMA((2,)), pltpu.SemaphoreType.DMA(())]),
        out_shape=jax.ShapeDtypeStruct((n_tok, d), F32),
        compiler_params=pltpu.CompilerParams(dimension_semantics=("arbitrary",), vmem_limit_bytes=VMEM_LIMIT),
        name="route_combine_final",
    )(base, cnt, slot, gate, x1, ga2, g_post2.reshape(1, d), ye)


def _moe(x1, h2, aff_t, ga2, g_post2, wg, wu, wd, tm_expert=512):
    b, s, d = x1.shape
    n_tok = b * s
    n_tiles = n_tok // TT
    cap = CAPACITY_FACTOR * n_tok // N_EXPERTS
    thr = _thresholds(aff_t, cap)
    slot_t, slot, gate, cnt = _select(aff_t, thr)
    cnt = cnt[:, :, 0]
    padded = -(-cnt // SEG_ALIGN) * SEG_ALIGN
    base = jnp.cumsum(padded, axis=0) - padded
    total = jnp.sum(padded, axis=0)
    rows = cap + SEG_ALIGN * n_tiles + WIN
    rows = -(-rows // tm_expert) * tm_expert
    base, cnt = base.reshape(-1).astype(jnp.int32), cnt.reshape(-1).astype(jnp.int32)
    xe = _dispatch(base, cnt, slot_t, h2.reshape(n_tok, d), rows)
    ye = _experts(total.astype(jnp.int32), xe, wg, wu, wd, tm=tm_expert)
    out = _combine(base, cnt, slot, gate, x1.reshape(n_tok, d), ga2, g_post2, ye, s // TT)
    return out.reshape(b, s, d)


def _rope_tables(s):
    half = D_HEAD // 2
    inv = 1.0 / (ROPE_BASE ** (jnp.arange(half, dtype=F32) / half))
    ang = jnp.arange(s, dtype=F32)[:, None] * inv[None, :]
    cos, sin = jnp.cos(ang), jnp.sin(ang)
    return jnp.concatenate([cos, cos], axis=-1), jnp.concatenate([-sin, sin], axis=-1)


def _layer(x, mod, p):
    b, s, d = x.shape
    sh1, sc1, ga1, sh2, sc2, ga2 = [m[:, None, :] for m in jnp.split(mod, 6, axis=-1)]
    cos2, sin2 = _rope_tables(s)
    u = _inproj(x, sc1, sh1, p["g_pre1"], cos2, sin2, p["w_in"])
    y_ret = _retention(u, p["log_gamma"], p["gn_ret"])
    y_hg = _hgrn2(u, p["lb"], p["gn_hg"], p["hg_consts"])
    x1, h2, aff_t = _merge(x, y_ret, y_hg, u, ga1, sc2, sh2, p["g_post1"], p["g_pre2"],
                           p["w_br_ret"], p["w_br_hg"], p["w_out"], p["w_router_t"])
    return _moe(x1, h2, aff_t, ga2, p["g_post2"], p["w_gate"], p["w_up"], p["w_down"])


def kernel(x_prompt, x_sample, c_prompt, c_sample, w_ada, b_ada, g_pre1, g_post1, w_in, gn_ret, gn_hg, lb_logits, w_br_ret, w_br_hg, w_out, g_pre2, g_post2, w_router, w_gate, w_up, w_down):
    lower_bounds = jnp.cumsum(jax.nn.softmax(lb_logits.astype(F32), axis=0), axis=0)
    log_gamma = jnp.log(1.0 - jnp.power(2.0, -5.0 - jnp.arange(N_HEADS, dtype=F32)))
    consts = _hg_constants()
    n_gate = 2 * D_MODEL
    xp, xs = x_prompt, x_sample
    bp, bs = xp.shape[0], xs.shape[0]
    for l in range(w_ada.shape[0]):
        c_all = jnp.concatenate([c_prompt, c_sample, jnp.zeros((-(bp + bs) % 8, D_MODEL), F32)], axis=0)
        mod = _ada(c_all, w_ada[l], b_ada[l])
        w_in_l = w_in[l]
        p = dict(
            g_pre1=g_pre1[l], g_post1=g_post1[l], g_pre2=g_pre2[l], g_post2=g_post2[l],
            w_in=jnp.concatenate([w_in_l[:, -n_gate:], w_in_l[:, :-n_gate]], axis=1).astype(BF16),
            log_gamma=jnp.broadcast_to(log_gamma[:, None, None], (N_HEADS, 1, LANES)),
            gn_ret=gn_ret[l], gn_hg=gn_hg[l], lb=lower_bounds[l], hg_consts=consts,
            w_br_ret=w_br_ret[l].astype(BF16), w_br_hg=w_br_hg[l].astype(BF16), w_out=w_out[l].astype(BF16),
            w_router_t=w_router[l].T,
            w_gate=w_gate[l].astype(BF16), w_up=w_up[l].astype(BF16), w_down=w_down[l].astype(BF16),
        )
        xp = _layer(xp, mod[:bp], p)
        xs = _layer(xs, mod[bp:bp + bs], p)
    return (xp, xs)
```

```python
import functools

import numpy as np
import jax
import jax.numpy as jnp
from jax import lax
from jax.experimental import pallas as pl
from jax.experimental.pallas import tpu as pltpu

F32 = jnp.float32
BF16 = jnp.bfloat16

D_MODEL = 1024
N_HEADS = 4
D_HEAD = 128
N_EXPERTS = 16
CAPACITY_FACTOR = 2
D_EXPERT = 2048
ROPE_BASE = 10000.0
NORM_EPS = 1e-6
CHUNK = 128
LANES = 128
VMEM_LIMIT = 56 * 1024 * 1024

D_IN = 2 * D_MODEL + 9 * N_HEADS * D_HEAD
COL_RETQ, COL_RETK, COL_RETV, COL_RETG = 16, 20, 24, 28
COL_HGQ, COL_HGI, COL_HGFF, COL_HGFB, COL_HGG = 32, 36, 40, 44, 48
TN_IN = 512
ROPE_CHUNKS = (COL_RETQ * LANES // TN_IN, COL_RETK * LANES // TN_IN)

LOG2_E = 1.4426950408889634
NT_DIMS = (((1,), (1,)), ((), ()))
TN_DIMS = (((0,), (0,)), ((), ()))


def _sigmoid(x):
    return 1.0 / (1.0 + jnp.exp(-x))


def _rms(x, eps=NORM_EPS):
    return x * lax.rsqrt(jnp.mean(x * x, axis=-1, keepdims=True) + eps)


def _ada_kernel(c_ref, w_ref, b_ref, o_ref):
    c = c_ref[...]
    o_ref[...] = jnp.dot(c * _sigmoid(c), w_ref[...], preferred_element_type=F32,
                         precision=lax.Precision.HIGHEST) + b_ref[...]


def _ada(c_all, w_ada, b_ada):
    rows = c_all.shape[0]
    n_out = w_ada.shape[1]
    tn = 1024
    return pl.pallas_call(
        _ada_kernel,
        grid=(n_out // tn,),
        in_specs=[pl.BlockSpec((rows, D_MODEL), lambda n: (0, 0)),
                  pl.BlockSpec((D_MODEL, tn), lambda n: (0, n)),
                  pl.BlockSpec((1, tn), lambda n: (0, n))],
        out_specs=pl.BlockSpec((rows, tn), lambda n: (0, n)),
        out_shape=jax.ShapeDtypeStruct((rows, n_out), F32),
        name="ada_mod",
    )(c_all, w_ada, b_ada.reshape(1, n_out))


def _inproj_kernel(x_ref, sc_ref, sh_ref, g_ref, cos_ref, sin_ref, w_ref, u_ref):
    h = (_rms(x_ref[0]) * (g_ref[...] * (1.0 + sc_ref[0])) + sh_ref[0]).astype(BF16)
    for n in range(D_IN // TN_IN):
        acc = jnp.dot(h, w_ref[:, n * TN_IN:(n + 1) * TN_IN], preferred_element_type=F32)
        if n in ROPE_CHUNKS:
            scale = 1.0 if n == ROPE_CHUNKS[0] else D_HEAD ** -0.5
            for hd in range(TN_IN // D_HEAD):
                blk = acc[:, hd * D_HEAD:(hd + 1) * D_HEAD]
                rot = blk * cos_ref[...] + pltpu.roll(blk, D_HEAD // 2, axis=1) * sin_ref[...]
                lo = n * TN_IN + hd * D_HEAD
                u_ref[0, :, lo:lo + D_HEAD] = (rot * scale).astype(BF16)
        else:
            u_ref[0, :, n * TN_IN:(n + 1) * TN_IN] = acc.astype(BF16)


def _inproj(x, sc, sh, g, cos2, sin2, w_bf16, tm=512):
    b, s, d = x.shape
    return pl.pallas_call(
        _inproj_kernel,
        grid=(b, s // tm),
        in_specs=[pl.BlockSpec((1, tm, d), lambda i, j: (i, j, 0)),
                  pl.BlockSpec((1, 1, d), lambda i, j: (i, 0, 0)),
                  pl.BlockSpec((1, 1, d), lambda i, j: (i, 0, 0)),
                  pl.BlockSpec((1, d), lambda i, j: (0, 0)),
                  pl.BlockSpec((tm, D_HEAD), lambda i, j: (j, 0)),
                  pl.BlockSpec((tm, D_HEAD), lambda i, j: (j, 0)),
                  pl.BlockSpec((d, D_IN), lambda i, j: (0, 0))],
        out_specs=pl.BlockSpec((1, tm, D_IN), lambda i, j: (i, j, 0)),
        out_shape=jax.ShapeDtypeStruct((b, s, D_IN), BF16),
        compiler_params=pltpu.CompilerParams(dimension_semantics=("parallel", "parallel"),
                                             vmem_limit_bytes=VMEM_LIMIT),
        name="inproj",
    )(x, sc, sh, g.reshape(1, d), cos2, sin2, w_bf16)


def _ret_kernel(q_ref, k_ref, v_ref, g_ref, lg_ref, gn_ref, o_ref, yf_ref, yb_ref, *, n_chunks):
    c = CHUNK
    lg = lg_ref[0]
    pos = lax.broadcasted_iota(jnp.int32, (c, LANES), 0).astype(F32)
    dist = jnp.abs(lax.broadcasted_iota(jnp.int32, (c, c), 0)
                   - lax.broadcasted_iota(jnp.int32, (c, c), 1)).astype(F32)
    decay = jnp.exp(lg * dist)
    q_fwd = jnp.exp(lg * (pos + 1.0)).astype(BF16)
    k_fwd = jnp.exp(lg * (c - 1.0 - pos)).astype(BF16)
    q_bwd = jnp.exp(lg * (c - pos)).astype(BF16)
    k_bwd = jnp.exp(lg * pos).astype(BF16)
    g_chunk = jnp.exp(lg * c)

    def scaled(t, f):
        return t * f

    def sweep(n, carry):
        r_f, r_b = carry
        sl = pl.ds(pl.multiple_of(n * c, c), c)
        q, k, v = q_ref[0, sl, :], k_ref[0, sl, :], v_ref[0, sl, :]
        s = lax.dot_general(q, k, NT_DIMS, preferred_element_type=F32) * decay
        lhs = jnp.concatenate([s.astype(BF16), scaled(q, q_fwd)], axis=1)
        rhs = jnp.concatenate([v, r_f.astype(BF16)], axis=0)
        yf_ref[sl, :] = jnp.dot(lhs, rhs, preferred_element_type=F32)
        r_f = g_chunk * r_f + lax.dot_general(scaled(k, k_fwd), v, TN_DIMS, preferred_element_type=F32)
        sl = pl.ds(pl.multiple_of((n_chunks - 1 - n) * c, c), c)
        q, k, v = q_ref[0, sl, :], k_ref[0, sl, :], v_ref[0, sl, :]
        yb_ref[sl, :] = jnp.dot(scaled(q, q_bwd), r_b.astype(BF16), preferred_element_type=F32)
        r_b = g_chunk * r_b + lax.dot_general(scaled(k, k_bwd), v, TN_DIMS, preferred_element_type=F32)
        return r_f, r_b

    zero = jnp.zeros((D_HEAD, D_HEAD), F32)
    lax.fori_loop(0, n_chunks, sweep, (zero, zero), unroll=4)

    def finish(n, carry):
        sl = pl.ds(pl.multiple_of(n * c, c), c)
        y = yf_ref[sl, :] + yb_ref[sl, :]
        yc = y - jnp.mean(y, axis=-1, keepdims=True)
        g = g_ref[0, sl, :].astype(F32)
        o_ref[0, sl, :] = (_rms(yc) * gn_ref[0] * (g * _sigmoid(g))).astype(BF16)
        return carry

    lax.fori_loop(0, n_chunks, finish, 0, unroll=4)


def _retention(u, log_gamma, gn):
    b, s, _ = u.shape

    def col(c0):
        return pl.BlockSpec((1, s, D_HEAD), lambda i, h: (i, 0, c0 + h))

    per_head = pl.BlockSpec((1, 1, D_HEAD), lambda i, h: (h, 0, 0))
    return pl.pallas_call(
        functools.partial(_ret_kernel, n_chunks=s // CHUNK),
        grid=(b, N_HEADS),
        in_specs=[col(COL_RETQ), col(COL_RETK), col(COL_RETV), col(COL_RETG), per_head, per_head],
        out_specs=pl.BlockSpec((1, s, D_HEAD), lambda i, h: (i, 0, h)),
        out_shape=jax.ShapeDtypeStruct((b, s, N_HEADS * D_HEAD), BF16),
        scratch_shapes=[pltpu.VMEM((s, D_HEAD), F32), pltpu.VMEM((s, D_HEAD), F32)],
        compiler_params=pltpu.CompilerParams(dimension_semantics=("parallel", "parallel"),
                                             vmem_limit_bytes=VMEM_LIMIT),
        name="retention",
    )(u, u, u, u, log_gamma, gn.reshape(N_HEADS, 1, D_HEAD))


N_LEVELS = 8
ROW_PIECE = 16
N_SPLIT_LEVEL = 5


def _hg_constants():
    c = CHUNK
    p = np.arange(c)
    tril = (p[None, :] <= p[:, None]).astype(np.float64)
    masks = [np.eye(c)]
    for l in range(1, N_LEVELS):
        s, half = 2 ** l, 2 ** (l - 1)
        blk = p // s
        upper = (p % s) >= half
        masks.append(((blk[:, None] == blk[None, :]) & upper[:, None] & (~upper)[None, :]).astype(np.float64))
    mask_f = np.stack(masks)
    mask_b = np.stack([m.T for m in masks])
    tile2 = lambda m: np.concatenate([m, m], axis=1)
    return (jnp.asarray(tile2(tril), BF16), jnp.asarray(tile2(tril[::-1, ::-1]), BF16),
            jnp.asarray(mask_f, F32), jnp.asarray(mask_b, F32))


def _neg_block_offsets(cum, level, forward):
    s = 2 ** level
    half = s // 2
    blocks = cum.reshape(CHUNK // s, s, LANES)
    lower, upper = blocks[:, :half, :], blocks[:, half:, :]
    if forward:
        ref = blocks[:, half - 1:half, :]
        out = jnp.concatenate([ref - lower, upper - ref], axis=1)
    else:
        ref = blocks[:, half:half + 1, :]
        out = jnp.concatenate([lower - ref, ref - upper], axis=1)
    return out.reshape(CHUNK, LANES)


SUBLANES = 8


def _neg_small_offsets(cum, lf, level, forward):
    s = 2 ** level
    half = s // 2
    row = lax.broadcasted_iota(jnp.int32, (CHUNK // SUBLANES, SUBLANES, LANES), 1)
    in_lower = (row & half) == 0
    if level == 1:
        keep = jnp.logical_not(in_lower) if forward else in_lower
        return jnp.where(keep.reshape(CHUNK, LANES), lf, 0.0)
    rows = cum.reshape(CHUNK // SUBLANES, SUBLANES, LANES)
    first = half - 1 if forward else half
    ref = rows[:, first:first + 1, :]
    for blk in range(1, SUBLANES // s):
        r = blk * s + first
        ref = jnp.where(row < blk * s, ref, rows[:, r:r + 1, :])
    diff = rows - ref
    return jnp.where(in_lower == forward, -diff, diff).reshape(CHUNK, LANES)


def _hg_gates(z, lb, mat_ref):
    sg = _sigmoid(z)
    lf = jnp.log(lb + (1.0 - lb) * sg) * LOG2_E
    kb = ((1.0 - lb) * (1.0 - sg)).astype(BF16)
    hi = lf.astype(BF16)
    mid = (lf - hi.astype(F32)).astype(BF16)
    return kb, lf, jnp.dot(mat_ref[...], jnp.concatenate([hi, mid], axis=0), preferred_element_type=F32)


def _block_diag(a, b):
    zero = jnp.zeros_like(a)
    return jnp.concatenate([jnp.concatenate([a, zero], axis=1), jnp.concatenate([zero, b], axis=1)], axis=0)


def _hg_pair(fwd, bwd, lb, mats, masks):
    c = CHUNK
    dirs = (True, False)
    q = [fwd[0], bwd[0]]
    v = [fwd[1], bwd[1]]
    state = [fwd[3], bwd[3]]
    kb, lf, cum = zip(*[_hg_gates(d[2], lb, m) for d, m in zip((fwd, bwd), mats)])
    piece = lambda t, i: t[ROW_PIECE * i:ROW_PIECE * (i + 1)]
    n_pieces = c // ROW_PIECE
    halves = lambda p: (p[:, :c], p[:, c:])

    def paired_nt(lhs, rhs):
        return halves(lax.dot_general(jnp.concatenate(lhs, axis=1), _block_diag(*rhs), NT_DIMS,
                                      preferred_element_type=F32))

    def scale(l):
        neg = [_neg_small_offsets(cum[d], lf[d], l, dirs[d]) if 2 ** l <= SUBLANES
               else _neg_block_offsets(cum[d], l, dirs[d]) for d in range(2)]
        return [jnp.exp2(n).astype(BF16) for n in neg]

    a = [p * m[0] for p, m in zip(paired_nt(q, kb), masks)]
    for l in range(1, N_SPLIT_LEVEL):
        e = scale(l)
        p = paired_nt([q[d] * e[d] for d in range(2)], [kb[d] * e[d] for d in range(2)])
        a = [a[d] + p[d] * masks[d][l] for d in range(2)]
    a = [[piece(a[d], i) for i in range(n_pieces)] for d in range(2)]
    for l in range(N_SPLIT_LEVEL, N_LEVELS):
        e = scale(l)
        per_half = 2 ** (l - 1) // ROW_PIECE
        is_query = [[((i // per_half) % 2 == 1) == dirs[d] for i in range(n_pieces)] for d in range(2)]
        queries = [[i for i in range(n_pieces) if is_query[d][i]] for d in range(2)]
        q_rows = [jnp.concatenate([piece(q[d], i) * piece(e[d], i) for i in queries[d]], axis=0) for d in range(2)]
        k_rows = [jnp.concatenate([piece(kb[d], i) if is_query[d][i] else piece(kb[d], i) * piece(e[d], i)
                                   for i in range(n_pieces)], axis=0) for d in range(2)]
        p = paired_nt(q_rows, k_rows)
        for d in range(2):
            for n, i in enumerate(queries[d]):
                a[d][i] = a[d][i] + piece(p[d], n) * masks[d][l, ROW_PIECE * i:ROW_PIECE * (i + 1), :]
    a = [jnp.concatenate(a[d], axis=0).astype(BF16) for d in range(2)]
    y = halves(jnp.dot(jnp.concatenate(a, axis=1), _block_diag(*v), preferred_element_type=F32))
    carried = paired_nt([q[d] * jnp.exp2(cum[d]).astype(BF16) for d in range(2)],
                        [state[d].astype(BF16) for d in range(2)])
    out = []
    for d in range(2):
        total = cum[d][c - 1:c] if dirs[d] else cum[d][0:1]
        kd = kb[d] * jnp.exp2(total - cum[d]).astype(BF16)
        new_state = state[d] * jnp.exp2(total) + lax.dot_general(v[d], kd, TN_DIMS, preferred_element_type=F32)
        out.append((y[d] + carried[d], new_state))
    return out


def _hg_kernel(q_ref, i_ref, zf_ref, zb_ref, g_ref, lb_ref, gn_ref, matf_ref, matb_ref, maskf_ref, maskb_ref,
               o_ref, yf_ref, yb_ref, *, n_chunks):
    c = CHUNK
    lb = lb_ref[0]

    def sweep(n, carry):
        s_f, s_b = carry
        sf = pl.ds(pl.multiple_of(n * c, c), c)
        sb = pl.ds(pl.multiple_of((n_chunks - 1 - n) * c, c), c)
        (yf_ref[sf, :], s_f), (yb_ref[sb, :], s_b) = _hg_pair(
            (q_ref[0, sf, :], i_ref[0, sf, :], zf_ref[0, sf, :].astype(F32), s_f),
            (q_ref[0, sb, :], i_ref[0, sb, :], zb_ref[0, sb, :].astype(F32), s_b),
            lb, (matf_ref, matb_ref), (maskf_ref, maskb_ref))
        return s_f, s_b

    zero = jnp.zeros((D_HEAD, D_HEAD), F32)
    lax.fori_loop(0, n_chunks, sweep, (zero, zero), unroll=4)

    def finish(n, carry):
        sl = pl.ds(pl.multiple_of(n * c, c), c)
        g = g_ref[0, sl, :].astype(F32)
        o_ref[0, sl, :] = (_rms(yf_ref[sl, :] + yb_ref[sl, :]) * gn_ref[0] * (g * _sigmoid(g))).astype(BF16)
        return carry

    lax.fori_loop(0, n_chunks, finish, 0, unroll=4)


def _hgrn2(u, lb, gn, consts):
    b, s, _ = u.shape
    mat_f, mat_b, mask_f, mask_b = consts

    def col(c0):
        return pl.BlockSpec((1, s, D_HEAD), lambda i, h: (i, 0, c0 + h))

    def whole(a):
        return pl.BlockSpec(a.shape, lambda i, h: (0,) * a.ndim)

    per_head = pl.BlockSpec((1, 1, D_HEAD), lambda i, h: (h, 0, 0))
    return pl.pallas_call(
        functools.partial(_hg_kernel, n_chunks=s // CHUNK),
        grid=(b, N_HEADS),
        in_specs=[col(COL_HGQ), col(COL_HGI), col(COL_HGFF), col(COL_HGFB), col(COL_HGG), per_head, per_head,
                  whole(mat_f), whole(mat_b), whole(mask_f), whole(mask_b)],
        out_specs=pl.BlockSpec((1, s, D_HEAD), lambda i, h: (i, 0, h)),
        out_shape=jax.ShapeDtypeStruct((b, s, N_HEADS * D_HEAD), BF16),
        scratch_shapes=[pltpu.VMEM((s, D_HEAD), F32), pltpu.VMEM((s, D_HEAD), F32)],
        compiler_params=pltpu.CompilerParams(dimension_semantics=("parallel", "parallel"),
                                             vmem_limit_bytes=VMEM_LIMIT),
        name="hgrn2",
    )(u, u, u, u, u, lb.reshape(N_HEADS, 1, D_HEAD), gn.reshape(N_HEADS, 1, D_HEAD),
      mat_f, mat_b, mask_f, mask_b)


def _merge_kernel(x_ref, yr_ref, yh_ref, za_ref, zb_ref, ga1_ref, sc2_ref, sh2_ref, gpost1_ref, gpre2_ref,
                  wr_ref, wh_ref, wo_ref, wrt_ref, x1_ref, h2_ref, aff_ref):
    br = jnp.dot(yr_ref[0], wr_ref[...], preferred_element_type=F32)
    bh = jnp.dot(yh_ref[0], wh_ref[...], preferred_element_type=F32)
    merged = _sigmoid(za_ref[0].astype(F32)) * br + _sigmoid(zb_ref[0].astype(F32)) * bh
    o = jnp.dot(merged.astype(BF16), wo_ref[...], preferred_element_type=F32)
    x1 = x_ref[0] + _rms(o) * (ga1_ref[0] * gpost1_ref[...])
    x1_ref[0] = x1
    h2 = _rms(x1) * (gpre2_ref[...] * (1.0 + sc2_ref[0])) + sh2_ref[0]
    h2_hi = h2.astype(BF16)
    h2_ref[0] = h2_hi
    h2_lo = (h2 - h2_hi.astype(F32)).astype(BF16)
    w_parts = wrt_ref[...]
    by_hi = lax.dot_general(w_parts, h2_hi, NT_DIMS, preferred_element_type=F32)
    logits = (by_hi[:N_EXPERTS] + by_hi[N_EXPERTS:]
              + lax.dot_general(w_parts[:N_EXPERTS], h2_lo, NT_DIMS, preferred_element_type=F32))
    ex = jnp.exp(logits - jnp.max(logits, axis=0, keepdims=True))
    aff_ref[...] = ex / jnp.sum(ex, axis=0, keepdims=True)


def _merge(x, y_ret, y_hg, u, ga1, sc2, sh2, g_post1, g_pre2, w_br_ret, w_br_hg, w_out, w_router_t, tm=512):
    b, s, d = x.shape
    nj = s // tm
    tok = lambda w: pl.BlockSpec((1, tm, w), lambda i, j: (i, j, 0))
    per_b = pl.BlockSpec((1, 1, d), lambda i, j: (i, 0, 0))
    vec = pl.BlockSpec((1, d), lambda i, j: (0, 0))
    whole = lambda a: pl.BlockSpec(a.shape, lambda i, j: (0,) * a.ndim)
    return pl.pallas_call(
        _merge_kernel,
        grid=(b, nj),
        in_specs=[tok(d), tok(N_HEADS * D_HEAD), tok(N_HEADS * D_HEAD),
                  pl.BlockSpec((1, tm, d), lambda i, j: (i, j, 0)),
                  pl.BlockSpec((1, tm, d), lambda i, j: (i, j, 1)),
                  per_b, per_b, per_b, vec, vec,
                  whole(w_br_ret), whole(w_br_hg), whole(w_out), whole(w_router_t)],
        out_specs=[tok(d), tok(d), pl.BlockSpec((N_EXPERTS, tm), lambda i, j: (0, i * nj + j))],
        out_shape=[jax.ShapeDtypeStruct((b, s, d), F32), jax.ShapeDtypeStruct((b, s, d), BF16),
                   jax.ShapeDtypeStruct((N_EXPERTS, b * s), F32)],
        compiler_params=pltpu.CompilerParams(dimension_semantics=("parallel", "parallel"),
                                             vmem_limit_bytes=VMEM_LIMIT),
        name="merge_outproj_router",
    )(x, y_ret, y_hg, u, u, ga1, sc2, sh2, g_post1.reshape(1, d), g_pre2.reshape(1, d),
      w_br_ret, w_br_hg, w_out, w_router_t)


TF_EXPERT = 512


def _expert_kernel(total_ref, x_ref, wg_ref, wu_ref, wd_ref, o_ref, *, tm):
    live = pl.program_id(1) * tm < total_ref[pl.program_id(0)]

    @pl.when(live)
    def _():
        x = x_ref[0]
        acc = jnp.zeros((tm, D_MODEL), F32)
        for f in range(D_EXPERT // TF_EXPERT):
            sl = slice(f * TF_EXPERT, (f + 1) * TF_EXPERT)
            a = jnp.dot(x, wg_ref[0, :, sl], preferred_element_type=F32)
            up = jnp.dot(x, wu_ref[0, :, sl], preferred_element_type=F32)
            acc += jnp.dot((a * _sigmoid(a) * up).astype(BF16), wd_ref[0, sl, :], preferred_element_type=F32)
        o_ref[0] = acc.astype(BF16)

    @pl.when(jnp.logical_not(live))
    def _():
        o_ref[0] = jnp.zeros((tm, D_MODEL), BF16)


def _experts(total, xe, wg, wu, wd, tm=512):
    e, rows, d = xe.shape
    return pl.pallas_call(
        functools.partial(_expert_kernel, tm=tm),
        grid_spec=pltpu.PrefetchScalarGridSpec(
            num_scalar_prefetch=1,
            grid=(e, rows // tm),
            in_specs=[pl.BlockSpec((1, tm, d), lambda i, j, t: (i, j, 0)),
                      pl.BlockSpec((1, d, D_EXPERT), lambda i, j, t: (i, 0, 0)),
                      pl.BlockSpec((1, d, D_EXPERT), lambda i, j, t: (i, 0, 0)),
                      pl.BlockSpec((1, D_EXPERT, d), lambda i, j, t: (i, 0, 0))],
            out_specs=pl.BlockSpec((1, tm, d), lambda i, j, t: (i, j, 0))),
        out_shape=jax.ShapeDtypeStruct((e, rows, d), BF16),
        compiler_params=pltpu.CompilerParams(dimension_semantics=("parallel", "parallel"),
                                             vmem_limit_bytes=VMEM_LIMIT),
        name="expert_ffn",
    )(total, xe, wg, wu, wd)


TT = 512
WIN = 128
SEG_ALIGN = 16


MIN_NORMAL = 2.0 ** -126
MANTISSA_STEPS = 53


def _threshold_kernel(a_ref, o_ref, *, cap):
    def count_ge(x):
        return jnp.sum(jnp.where(a_ref[...] >= x, 1.0, 0.0), axis=1, keepdims=True)

    lo = jnp.where(count_ge(MIN_NORMAL) >= cap, MIN_NORMAL, 0.0) * jnp.ones((N_EXPERTS, 1), F32)
    for j in (64, 32, 16, 8, 4, 2, 1):
        cand = lo * (2.0 ** j)
        lo = jnp.where(count_ge(cand) >= cap, cand, lo)
    hi = jnp.where(lo > 0.0, 2.0 * lo, MIN_NORMAL)

    def halve(_, bounds):
        lo, hi = bounds
        mid = 0.5 * (lo + hi)
        ok = count_ge(mid) >= cap
        return jnp.where(ok, mid, lo), jnp.where(ok, hi, mid)

    lo, hi = lax.fori_loop(0, MANTISSA_STEPS, halve, (lo, hi))
    a = a_ref[...]
    thr = jnp.min(jnp.where(a >= lo, a, 2.0), axis=1, keepdims=True)
    n_gt = jnp.sum(jnp.where(a > thr, 1.0, 0.0), axis=1, keepdims=True)
    o_ref[0] = jnp.broadcast_to(thr, (N_EXPERTS, LANES))
    o_ref[1] = jnp.broadcast_to(cap - n_gt, (N_EXPERTS, LANES))


def _thresholds(aff_t, cap):
    return pl.pallas_call(
        functools.partial(_threshold_kernel, cap=cap),
        out_shape=jax.ShapeDtypeStruct((2, N_EXPERTS, LANES), F32),
        compiler_params=pltpu.CompilerParams(vmem_limit_bytes=VMEM_LIMIT),
        name="route_threshold",
    )(aff_t)


def _select_kernel(a_ref, thr_ref, upper_ref, slot_t_ref, slot_ref, gate_ref, cnt_ref, seen_ref):
    @pl.when(pl.program_id(0) == 0)
    def _():
        seen_ref[...] = jnp.zeros_like(seen_ref)

    a = a_ref[...]
    thr = thr_ref[0][:, 0:1]
    need = thr_ref[1][:, 0:1]
    gt, eq = a > thr, a == thr
    marks = jnp.concatenate([jnp.where(gt, 1.0, 0.0), jnp.where(eq, 1.0, 0.0)], axis=0).astype(BF16)
    before = jnp.dot(marks, upper_ref[...], preferred_element_type=F32)
    gt_before, eq_before = before[:N_EXPERTS], before[N_EXPERTS:]
    seen = seen_ref[:, 0:1]
    eq_rank = seen + eq_before
    sel = jnp.logical_or(gt, jnp.logical_and(eq, eq_rank < need))
    rank = gt_before + jnp.minimum(eq_rank, need) - jnp.minimum(seen, need)
    slot_t = jnp.where(sel, rank, -1.0)
    slot_t_ref[...] = slot_t.astype(jnp.int32)
    cnt_ref[0] = jnp.broadcast_to(jnp.sum(jnp.where(sel, 1, 0), axis=1, keepdims=True), (N_EXPERTS, LANES))
    seen_ref[...] = seen_ref[...] + jnp.sum(jnp.where(eq, 1.0, 0.0), axis=1, keepdims=True)
    pad = jnp.zeros((LANES - N_EXPERTS, TT), F32)
    slot_ref[...] = jnp.concatenate([slot_t, pad], axis=0).T[:, :N_EXPERTS].astype(jnp.int32)
    gate_ref[...] = jnp.concatenate([a, pad], axis=0).T[:, :N_EXPERTS]


def _select(aff_t, thr):
    n_tok = aff_t.shape[1]
    n_tiles = n_tok // TT
    t = np.arange(TT)
    upper = jnp.asarray(t[:, None] < t[None, :], BF16)
    return pl.pallas_call(
        _select_kernel,
        grid=(n_tiles,),
        in_specs=[pl.BlockSpec((N_EXPERTS, TT), lambda i: (0, i)),
                  pl.BlockSpec((2, N_EXPERTS, LANES), lambda i: (0, 0, 0)),
                  pl.BlockSpec((TT, TT), lambda i: (0, 0))],
        out_specs=[pl.BlockSpec((N_EXPERTS, TT), lambda i: (0, i)),
                   pl.BlockSpec((TT, N_EXPERTS), lambda i: (i, 0)),
                   pl.BlockSpec((TT, N_EXPERTS), lambda i: (i, 0)),
                   pl.BlockSpec((1, N_EXPERTS, LANES), lambda i: (i, 0, 0))],
        out_shape=[jax.ShapeDtypeStruct((N_EXPERTS, n_tok), jnp.int32),
                   jax.ShapeDtypeStruct((n_tok, N_EXPERTS), jnp.int32),
                   jax.ShapeDtypeStruct((n_tok, N_EXPERTS), F32),
                   jax.ShapeDtypeStruct((n_tiles, N_EXPERTS, LANES), jnp.int32)],
        scratch_shapes=[pltpu.VMEM((N_EXPERTS, LANES), F32)],
        compiler_params=pltpu.CompilerParams(dimension_semantics=("arbitrary",)),
        name="route_select",
    )(aff_t, thr, upper)


def _dispatch_kernel(base_ref, cnt_ref, slot_ref, h_ref, xe_in_ref, xe_ref, stage_ref, extra_ref, sem, extra_sem):
    del xe_in_ref
    i = pl.program_id(0)
    par = i % 2
    row = lax.broadcasted_iota(jnp.int32, (WIN, TT), 0)
    slot = slot_ref[...]
    onehot = jnp.concatenate([jnp.where(slot[e:e + 1, :] == row, 1.0, 0.0).astype(BF16)
                              for e in range(N_EXPERTS)], axis=0)
    stage_ref[par] = jnp.dot(onehot, h_ref[...], preferred_element_type=F32).astype(BF16)

    def window_copy(e, start):
        return pltpu.make_async_copy(stage_ref.at[par, pl.ds(e * WIN, WIN), :],
                                     xe_ref.at[e, pl.ds(pl.multiple_of(start, SEG_ALIGN), WIN), :], sem.at[e])

    for e in range(N_EXPERTS):
        start = base_ref[i * N_EXPERTS + e]

        @pl.when(i > 0)
        def _():
            window_copy(e, start).wait()

        window_copy(e, start).start()

    def extra_windows(e, carry):
        start = base_ref[i * N_EXPERTS + e]

        def one(w, c):
            srow = slot_ref[pl.ds(e, 1), :] - (w + 1) * WIN
            oh = jnp.where(srow == row, 1.0, 0.0).astype(BF16)
            extra_ref[...] = jnp.dot(oh, h_ref[...], preferred_element_type=F32).astype(BF16)
            cp = pltpu.make_async_copy(
                extra_ref, xe_ref.at[e, pl.ds(pl.multiple_of(start + (w + 1) * WIN, SEG_ALIGN), WIN), :], extra_sem)
            cp.start()
            cp.wait()
            return c

        return lax.fori_loop(0, jnp.maximum(cnt_ref[i * N_EXPERTS + e] - 1, 0) // WIN, one, carry)

    lax.fori_loop(0, N_EXPERTS, extra_windows, 0)

    @pl.when(i == pl.num_programs(0) - 1)
    def _():
        for e in range(N_EXPERTS):
            window_copy(e, base_ref[i * N_EXPERTS + e]).wait()


def _dispatch(base, cnt, slot_t, h2, rows):
    n_tok, d = h2.shape
    xe0 = jnp.zeros((N_EXPERTS, rows, d), BF16)
    return pl.pallas_call(
        _dispatch_kernel,
        grid_spec=pltpu.PrefetchScalarGridSpec(
            num_scalar_prefetch=2,
            grid=(n_tok // TT,),
            in_specs=[pl.BlockSpec((N_EXPERTS, TT), lambda i, b, c: (0, i)),
                      pl.BlockSpec((TT, d), lambda i, b, c: (i, 0)),
                      pl.BlockSpec(memory_space=pl.ANY)],
            out_specs=pl.BlockSpec(memory_space=pl.ANY),
            scratch_shapes=[pltpu.VMEM((2, N_EXPERTS * WIN, d), BF16), pltpu.VMEM((WIN, d), BF16),
                            pltpu.SemaphoreType.DMA((N_EXPERTS,)), pltpu.SemaphoreType.DMA(())]),
        out_shape=jax.ShapeDtypeStruct((N_EXPERTS, rows, d), BF16),
        input_output_aliases={4: 0},
        compiler_params=pltpu.CompilerParams(dimension_semantics=("arbitrary",), vmem_limit_bytes=VMEM_LIMIT),
        name="route_dispatch",
    )(base, cnt, slot_t, h2, xe0)


def _combine_kernel(base_ref, cnt_ref, slot_ref, gate_ref, x1_ref, ga2_ref, g_ref, ye_ref, o_ref,
                    ybuf, extra_ref, acc_ref, sem, extra_sem):
    i = pl.program_id(0)
    par = i % 2

    def window_copy(step, e, buf):
        start = pl.multiple_of(base_ref[step * N_EXPERTS + e], SEG_ALIGN)
        return pltpu.make_async_copy(ye_ref.at[e, pl.ds(start, WIN), :],
                                     ybuf.at[buf, pl.ds(e * WIN, WIN), :], sem.at[buf])

    @pl.when(i == 0)
    def _():
        for e in range(N_EXPERTS):
            window_copy(0, e, 0).start()

    @pl.when(i + 1 < pl.num_programs(0))
    def _():
        for e in range(N_EXPERTS):
            window_copy(i + 1, e, 1 - par).start()

    slot, gate = slot_ref[...], gate_ref[...]
    lane = lax.broadcasted_iota(jnp.int32, (TT, WIN), 1)
    weights = jnp.concatenate([jnp.where(slot[:, e:e + 1] == lane, gate[:, e:e + 1], 0.0).astype(BF16)
                               for e in range(N_EXPERTS)], axis=1)
    for e in range(N_EXPERTS):
        window_copy(i, e, par).wait()
    acc_ref[...] = jnp.dot(weights, ybuf[par], preferred_element_type=F32)

    def extra_windows(e, carry):
        start = base_ref[i * N_EXPERTS + e]

        def one(w, c):
            cp = pltpu.make_async_copy(
                ye_ref.at[e, pl.ds(pl.multiple_of(start + (w + 1) * WIN, SEG_ALIGN), WIN), :], extra_ref, extra_sem)
            cp.start()
            pick = lax.broadcasted_iota(jnp.int32, (TT, N_EXPERTS), 1) == e
            slot_e = jnp.sum(jnp.where(pick, slot_ref[...], 0), axis=1, keepdims=True)
            gate_e = jnp.sum(jnp.where(pick, gate_ref[...], 0.0), axis=1, keepdims=True)
            wts = jnp.where(slot_e - (w + 1) * WIN == lane, gate_e, 0.0).astype(BF16)
            cp.wait()
            acc_ref[...] += jnp.dot(wts, extra_ref[...], preferred_element_type=F32)
            return c

        return lax.fori_loop(0, jnp.maximum(cnt_ref[i * N_EXPERTS + e] - 1, 0) // WIN, one, carry)

    lax.fori_loop(0, N_EXPERTS, extra_windows, 0)
    o_ref[...] = x1_ref[...] + ga2_ref[0] * (_rms(acc_ref[...]) * g_ref[...])


def _combine(base, cnt, slot, gate, x1, ga2, g_post2, ye, tiles_per_seq):
    n_tok, d = x1.shape
    tok = lambda w: pl.BlockSpec((TT, w), lambda i, b, c: (i, 0))
    return pl.pallas_call(
        _combine_kernel,
        grid_spec=pltpu.PrefetchScalarGridSpec(
            num_scalar_prefetch=2,
            grid=(n_tok // TT,),
            in_specs=[tok(N_EXPERTS), tok(N_EXPERTS), tok(d),
                      pl.BlockSpec((1, 1, d), lambda i, b, c: (i // tiles_per_seq, 0, 0)),
                      pl.BlockSpec((1, d), lambda i, b, c: (0, 0)),
                      pl.BlockSpec(memory_space=pl.ANY)],
            out_specs=tok(d),
            scratch_shapes=[pltpu.VMEM((2, N_EXPERTS * WIN, d), BF16), pltpu.VMEM((WIN, d), BF16),
                            pltpu.VMEM((TT, d), F32),
                            pltpu.SemaphoreType.DMA((2,)), pltpu.SemaphoreType.DMA(())]),
        out_shape=jax.ShapeDtypeStruct((n_tok, d), F32),
        compiler_params=pltpu.CompilerParams(dimension_semantics=("arbitrary",), vmem_limit_bytes=VMEM_LIMIT),
        name="route_combine_final",
    )(base, cnt, slot, gate, x1, ga2, g_post2.reshape(1, d), ye)


def _moe(x1, h2, aff_t, ga2, g_post2, wg, wu, wd, tm_expert=512):
    b, s, d = x1.shape
    n_tok = b * s
    n_tiles = n_tok // TT
    cap = CAPACITY_FACTOR * n_tok // N_EXPERTS
    thr = _thresholds(aff_t, cap)
    slot_t, slot, gate, cnt = _select(aff_t, thr)
    cnt = cnt[:, :, 0]
    padded = -(-cnt // SEG_ALIGN) * SEG_ALIGN
    base = jnp.cumsum(padded, axis=0) - padded
    total = jnp.sum(padded, axis=0)
    rows = cap + SEG_ALIGN * n_tiles + WIN
    rows = -(-rows // tm_expert) * tm_expert
    base, cnt = base.reshape(-1).astype(jnp.int32), cnt.reshape(-1).astype(jnp.int32)
    xe = _dispatch(base, cnt, slot_t, h2.reshape(n_tok, d), rows)
    ye = _experts(total.astype(jnp.int32), xe, wg, wu, wd, tm=tm_expert)
    out = _combine(base, cnt, slot, gate, x1.reshape(n_tok, d), ga2, g_post2, ye, s // TT)
    return out.reshape(b, s, d)


def _hi_lo_rows(w):
    hi = w.astype(BF16)
    return jnp.concatenate([hi, (w - hi.astype(F32)).astype(BF16)], axis=0)


def _rope_tables(s):
    half = D_HEAD // 2
    inv = 1.0 / (ROPE_BASE ** (jnp.arange(half, dtype=F32) / half))
    ang = jnp.arange(s, dtype=F32)[:, None] * inv[None, :]
    cos, sin = jnp.cos(ang), jnp.sin(ang)
    return jnp.concatenate([cos, cos], axis=-1), jnp.concatenate([-sin, sin], axis=-1)


def _layer(x, mod, p):
    b, s, d = x.shape
    sh1, sc1, ga1, sh2, sc2, ga2 = [m[:, None, :] for m in jnp.split(mod, 6, axis=-1)]
    cos2, sin2 = _rope_tables(s)
    u = _inproj(x, sc1, sh1, p["g_pre1"], cos2, sin2, p["w_in"])
    y_ret = _retention(u, p["log_gamma"], p["gn_ret"])
    y_hg = _hgrn2(u, p["lb"], p["gn_hg"], p["hg_consts"])
    x1, h2, aff_t = _merge(x, y_ret, y_hg, u, ga1, sc2, sh2, p["g_post1"], p["g_pre2"],
                           p["w_br_ret"], p["w_br_hg"], p["w_out"], p["w_router_t"])
    return _moe(x1, h2, aff_t, ga2, p["g_post2"], p["w_gate"], p["w_up"], p["w_down"])


def kernel(x_prompt, x_sample, c_prompt, c_sample, w_ada, b_ada, g_pre1, g_post1, w_in, gn_ret, gn_hg, lb_logits, w_br_ret, w_br_hg, w_out, g_pre2, g_post2, w_router, w_gate, w_up, w_down):
    lower_bounds = jnp.cumsum(jax.nn.softmax(lb_logits.astype(F32), axis=0), axis=0)
    log_gamma = jnp.log(1.0 - jnp.power(2.0, -5.0 - jnp.arange(N_HEADS, dtype=F32)))
    consts = _hg_constants()
    n_gate = 2 * D_MODEL
    xp, xs = x_prompt, x_sample
    bp, bs = xp.shape[0], xs.shape[0]
    for l in range(w_ada.shape[0]):
        c_all = jnp.concatenate([c_prompt, c_sample, jnp.zeros((-(bp + bs) % 8, D_MODEL), F32)], axis=0)
        mod = _ada(c_all, w_ada[l], b_ada[l])
        w_in_l = w_in[l]
        p = dict(
            g_pre1=g_pre1[l], g_post1=g_post1[l], g_pre2=g_pre2[l], g_post2=g_post2[l],
            w_in=jnp.concatenate([w_in_l[:, -n_gate:], w_in_l[:, :-n_gate]], axis=1).astype(BF16),
            log_gamma=jnp.broadcast_to(log_gamma[:, None, None], (N_HEADS, 1, LANES)),
            gn_ret=gn_ret[l], gn_hg=gn_hg[l], lb=lower_bounds[l], hg_consts=consts,
            w_br_ret=w_br_ret[l].astype(BF16), w_br_hg=w_br_hg[l].astype(BF16), w_out=w_out[l].astype(BF16),
            w_router_t=_hi_lo_rows(w_router[l].T),
            w_gate=w_gate[l].astype(BF16), w_up=w_up[l].astype(BF16), w_down=w_down[l].astype(BF16),
        )
        xp = _layer(xp, mod[:bp], p)
        xs = _layer(xs, mod[bp:bp + bs], p)
    return (xp, xs)
```

```python
import functools

import numpy as np
import jax
import jax.numpy as jnp
from jax import lax
from jax.experimental import pallas as pl
from jax.experimental.pallas import tpu as pltpu

F32 = jnp.float32
BF16 = jnp.bfloat16

D_MODEL = 1024
N_HEADS = 4
D_HEAD = 128
N_EXPERTS = 16
CAPACITY_FACTOR = 2
D_EXPERT = 2048
ROPE_BASE = 10000.0
NORM_EPS = 1e-6
CHUNK = 128
LANES = 128
VMEM_LIMIT = 56 * 1024 * 1024

D_IN = 2 * D_MODEL + 9 * N_HEADS * D_HEAD
COL_RETQ, COL_RETK, COL_RETV, COL_RETG = 16, 20, 24, 28
COL_HGQ, COL_HGI, COL_HGFF, COL_HGFB, COL_HGG = 32, 36, 40, 44, 48
TN_IN = 512
ROPE_CHUNKS = (COL_RETQ * LANES // TN_IN, COL_RETK * LANES // TN_IN)

LOG2_E = 1.4426950408889634
NT_DIMS = (((1,), (1,)), ((), ()))
TN_DIMS = (((0,), (0,)), ((), ()))


def _sigmoid(x):
    return 1.0 / (1.0 + jnp.exp(-x))


def _rms(x, eps=NORM_EPS):
    return x * lax.rsqrt(jnp.mean(x * x, axis=-1, keepdims=True) + eps)


def _ada_kernel(c_ref, w_ref, b_ref, o_ref):
    c = c_ref[...]
    o_ref[...] = jnp.dot(c * _sigmoid(c), w_ref[...], preferred_element_type=F32,
                         precision=lax.Precision.HIGHEST) + b_ref[...]


def _ada(c_all, w_ada, b_ada):
    rows = c_all.shape[0]
    n_out = w_ada.shape[1]
    tn = 1024
    return pl.pallas_call(
        _ada_kernel,
        grid=(n_out // tn,),
        in_specs=[pl.BlockSpec((rows, D_MODEL), lambda n: (0, 0)),
                  pl.BlockSpec((D_MODEL, tn), lambda n: (0, n)),
                  pl.BlockSpec((1, tn), lambda n: (0, n))],
        out_specs=pl.BlockSpec((rows, tn), lambda n: (0, n)),
        out_shape=jax.ShapeDtypeStruct((rows, n_out), F32),
        name="ada_mod",
    )(c_all, w_ada, b_ada.reshape(1, n_out))


def _inproj_kernel(x_ref, sc_ref, sh_ref, g_ref, cos_ref, sin_ref, w_ref, u_ref):
    h = (_rms(x_ref[0]) * (g_ref[...] * (1.0 + sc_ref[0])) + sh_ref[0]).astype(BF16)
    for n in range(D_IN // TN_IN):
        acc = jnp.dot(h, w_ref[:, n * TN_IN:(n + 1) * TN_IN], preferred_element_type=F32)
        if n in ROPE_CHUNKS:
            scale = 1.0 if n == ROPE_CHUNKS[0] else D_HEAD ** -0.5
            for hd in range(TN_IN // D_HEAD):
                blk = acc[:, hd * D_HEAD:(hd + 1) * D_HEAD]
                rot = blk * cos_ref[...] + pltpu.roll(blk, D_HEAD // 2, axis=1) * sin_ref[...]
                lo = n * TN_IN + hd * D_HEAD
                u_ref[0, :, lo:lo + D_HEAD] = (rot * scale).astype(BF16)
        else:
            u_ref[0, :, n * TN_IN:(n + 1) * TN_IN] = acc.astype(BF16)


def _inproj(x, sc, sh, g, cos2, sin2, w_bf16, tm=512):
    b, s, d = x.shape
    return pl.pallas_call(
        _inproj_kernel,
        grid=(b, s // tm),
        in_specs=[pl.BlockSpec((1, tm, d), lambda i, j: (i, j, 0)),
                  pl.BlockSpec((1, 1, d), lambda i, j: (i, 0, 0)),
                  pl.BlockSpec((1, 1, d), lambda i, j: (i, 0, 0)),
                  pl.BlockSpec((1, d), lambda i, j: (0, 0)),
                  pl.BlockSpec((tm, D_HEAD), lambda i, j: (j, 0)),
                  pl.BlockSpec((tm, D_HEAD), lambda i, j: (j, 0)),
                  pl.BlockSpec((d, D_IN), lambda i, j: (0, 0))],
        out_specs=pl.BlockSpec((1, tm, D_IN), lambda i, j: (i, j, 0)),
        out_shape=jax.ShapeDtypeStruct((b, s, D_IN), BF16),
        compiler_params=pltpu.CompilerParams(dimension_semantics=("parallel", "parallel"),
                                             vmem_limit_bytes=VMEM_LIMIT),
        name="inproj",
    )(x, sc, sh, g.reshape(1, d), cos2, sin2, w_bf16)


def _ret_kernel(q_ref, k_ref, v_ref, g_ref, lg_ref, gn_ref, o_ref, yf_ref, yb_ref, *, n_chunks):
    c = CHUNK
    lg = lg_ref[0]
    pos = lax.broadcasted_iota(jnp.int32, (c, LANES), 0).astype(F32)
    dist = jnp.abs(lax.broadcasted_iota(jnp.int32, (c, c), 0)
                   - lax.broadcasted_iota(jnp.int32, (c, c), 1)).astype(F32)
    decay = jnp.exp(lg * dist)
    q_fwd = jnp.exp(lg * (pos + 1.0)).astype(BF16)
    k_fwd = jnp.exp(lg * (c - 1.0 - pos)).astype(BF16)
    q_bwd = jnp.exp(lg * (c - pos)).astype(BF16)
    k_bwd = jnp.exp(lg * pos).astype(BF16)
    g_chunk = jnp.exp(lg * c)

    def scaled(t, f):
        return t * f

    def sweep(n, carry):
        r_f, r_b = carry
        sl = pl.ds(pl.multiple_of(n * c, c), c)
        q, k, v = q_ref[0, sl, :], k_ref[0, sl, :], v_ref[0, sl, :]
        s = lax.dot_general(q, k, NT_DIMS, preferred_element_type=F32) * decay
        lhs = jnp.concatenate([s.astype(BF16), scaled(q, q_fwd)], axis=1)
        rhs = jnp.concatenate([v, r_f.astype(BF16)], axis=0)
        yf_ref[sl, :] = jnp.dot(lhs, rhs, preferred_element_type=F32)
        r_f = g_chunk * r_f + lax.dot_general(scaled(k, k_fwd), v, TN_DIMS, preferred_element_type=F32)
        sl = pl.ds(pl.multiple_of((n_chunks - 1 - n) * c, c), c)
        q, k, v = q_ref[0, sl, :], k_ref[0, sl, :], v_ref[0, sl, :]
        yb_ref[sl, :] = jnp.dot(scaled(q, q_bwd), r_b.astype(BF16), preferred_element_type=F32)
        r_b = g_chunk * r_b + lax.dot_general(scaled(k, k_bwd), v, TN_DIMS, preferred_element_type=F32)
        return r_f, r_b

    zero = jnp.zeros((D_HEAD, D_HEAD), F32)
    lax.fori_loop(0, n_chunks, sweep, (zero, zero), unroll=4)

    def finish(n, carry):
        sl = pl.ds(pl.multiple_of(n * c, c), c)
        y = yf_ref[sl, :] + yb_ref[sl, :]
        yc = y - jnp.mean(y, axis=-1, keepdims=True)
        g = g_ref[0, sl, :].astype(F32)
        o_ref[0, sl, :] = (_rms(yc) * gn_ref[0] * (g * _sigmoid(g))).astype(BF16)
        return carry

    lax.fori_loop(0, n_chunks, finish, 0, unroll=4)


def _retention(u, log_gamma, gn):
    b, s, _ = u.shape

    def col(c0):
        return pl.BlockSpec((1, s, D_HEAD), lambda i, h: (i, 0, c0 + h))

    per_head = pl.BlockSpec((1, 1, D_HEAD), lambda i, h: (h, 0, 0))
    return pl.pallas_call(
        functools.partial(_ret_kernel, n_chunks=s // CHUNK),
        grid=(b, N_HEADS),
        in_specs=[col(COL_RETQ), col(COL_RETK), col(COL_RETV), col(COL_RETG), per_head, per_head],
        out_specs=pl.BlockSpec((1, s, D_HEAD), lambda i, h: (i, 0, h)),
        out_shape=jax.ShapeDtypeStruct((b, s, N_HEADS * D_HEAD), BF16),
        scratch_shapes=[pltpu.VMEM((s, D_HEAD), F32), pltpu.VMEM((s, D_HEAD), F32)],
        compiler_params=pltpu.CompilerParams(dimension_semantics=("parallel", "parallel"),
                                             vmem_limit_bytes=VMEM_LIMIT),
        name="retention",
    )(u, u, u, u, log_gamma, gn.reshape(N_HEADS, 1, D_HEAD))


N_LEVELS = 8
ROW_PIECE = 16
N_SPLIT_LEVEL = 5


def _hg_constants():
    c = CHUNK
    p = np.arange(c)
    tril = (p[None, :] <= p[:, None]).astype(np.float64)
    masks = [np.eye(c)]
    for l in range(1, N_LEVELS):
        s, half = 2 ** l, 2 ** (l - 1)
        blk = p // s
        upper = (p % s) >= half
        masks.append(((blk[:, None] == blk[None, :]) & upper[:, None] & (~upper)[None, :]).astype(np.float64))
    mask_f = np.stack(masks)
    mask_b = np.stack([m.T for m in masks])
    tile2 = lambda m: np.concatenate([m, m], axis=1)
    return (jnp.asarray(tile2(tril), BF16), jnp.asarray(tile2(tril[::-1, ::-1]), BF16),
            jnp.asarray(mask_f, F32), jnp.asarray(mask_b, F32))


def _neg_block_offsets(cum, level, forward):
    s = 2 ** level
    half = s // 2
    blocks = cum.reshape(CHUNK // s, s, LANES)
    lower, upper = blocks[:, :half, :], blocks[:, half:, :]
    if forward:
        ref = blocks[:, half - 1:half, :]
        out = jnp.concatenate([ref - lower, upper - ref], axis=1)
    else:
        ref = blocks[:, half:half + 1, :]
        out = jnp.concatenate([lower - ref, ref - upper], axis=1)
    return out.reshape(CHUNK, LANES)


SUBLANES = 8


def _neg_small_offsets(cum, lf, level, forward):
    s = 2 ** level
    half = s // 2
    row = lax.broadcasted_iota(jnp.int32, (CHUNK // SUBLANES, SUBLANES, LANES), 1)
    in_lower = (row & half) == 0
    if level == 1:
        keep = jnp.logical_not(in_lower) if forward else in_lower
        return jnp.where(keep.reshape(CHUNK, LANES), lf, 0.0)
    rows = cum.reshape(CHUNK // SUBLANES, SUBLANES, LANES)
    first = half - 1 if forward else half
    ref = rows[:, first:first + 1, :]
    for blk in range(1, SUBLANES // s):
        r = blk * s + first
        ref = jnp.where(row < blk * s, ref, rows[:, r:r + 1, :])
    diff = rows - ref
    return jnp.where(in_lower == forward, -diff, diff).reshape(CHUNK, LANES)


def _hg_gates(z, lb, mat_ref):
    sg = _sigmoid(z)
    lf = jnp.log(lb + (1.0 - lb) * sg) * LOG2_E
    kb = ((1.0 - lb) * (1.0 - sg)).astype(BF16)
    hi = lf.astype(BF16)
    mid = (lf - hi.astype(F32)).astype(BF16)
    return kb, lf, jnp.dot(mat_ref[...], jnp.concatenate([hi, mid], axis=0), preferred_element_type=F32)


def _block_diag(a, b):
    zero = jnp.zeros_like(a)
    return jnp.concatenate([jnp.concatenate([a, zero], axis=1), jnp.concatenate([zero, b], axis=1)], axis=0)


def _hg_pair(fwd, bwd, lb, mats, masks):
    c = CHUNK
    dirs = (True, False)
    q = [fwd[0], bwd[0]]
    v = [fwd[1], bwd[1]]
    state = [fwd[3], bwd[3]]
    kb, lf, cum = zip(*[_hg_gates(d[2], lb, m) for d, m in zip((fwd, bwd), mats)])
    piece = lambda t, i: t[ROW_PIECE * i:ROW_PIECE * (i + 1)]
    n_pieces = c // ROW_PIECE
    halves = lambda p: (p[:, :c], p[:, c:])

    def paired_nt(lhs, rhs):
        return halves(lax.dot_general(jnp.concatenate(lhs, axis=1), _block_diag(*rhs), NT_DIMS,
                                      preferred_element_type=F32))

    def scale(l):
        neg = [_neg_small_offsets(cum[d], lf[d], l, dirs[d]) if 2 ** l <= SUBLANES
               else _neg_block_offsets(cum[d], l, dirs[d]) for d in range(2)]
        return [jnp.exp2(n).astype(BF16) for n in neg]

    a = [p * m[0] for p, m in zip(paired_nt(q, kb), masks)]
    for l in range(1, N_SPLIT_LEVEL):
        e = scale(l)
        p = paired_nt([q[d] * e[d] for d in range(2)], [kb[d] * e[d] for d in range(2)])
        a = [a[d] + p[d] * masks[d][l] for d in range(2)]
    a = [[piece(a[d], i) for i in range(n_pieces)] for d in range(2)]
    for l in range(N_SPLIT_LEVEL, N_LEVELS):
        e = scale(l)
        per_half = 2 ** (l - 1) // ROW_PIECE
        is_query = [[((i // per_half) % 2 == 1) == dirs[d] for i in range(n_pieces)] for d in range(2)]
        queries = [[i for i in range(n_pieces) if is_query[d][i]] for d in range(2)]
        q_rows = [jnp.concatenate([piece(q[d], i) * piece(e[d], i) for i in queries[d]], axis=0) for d in range(2)]
        k_rows = [jnp.concatenate([piece(kb[d], i) if is_query[d][i] else piece(kb[d], i) * piece(e[d], i)
                                   for i in range(n_pieces)], axis=0) for d in range(2)]
        p = paired_nt(q_rows, k_rows)
        for d in range(2):
            for n, i in enumerate(queries[d]):
                a[d][i] = a[d][i] + piece(p[d], n) * masks[d][l, ROW_PIECE * i:ROW_PIECE * (i + 1), :]
    a = [jnp.concatenate(a[d], axis=0).astype(BF16) for d in range(2)]
    y = halves(jnp.dot(jnp.concatenate(a, axis=1), _block_diag(*v), preferred_element_type=F32))
    carried = paired_nt([q[d] * jnp.exp2(cum[d]).astype(BF16) for d in range(2)],
                        [state[d].astype(BF16) for d in range(2)])
    out = []
    for d in range(2):
        total = cum[d][c - 1:c] if dirs[d] else cum[d][0:1]
        kd = kb[d] * jnp.exp2(total - cum[d]).astype(BF16)
        new_state = state[d] * jnp.exp2(total) + lax.dot_general(v[d], kd, TN_DIMS, preferred_element_type=F32)
        out.append((y[d] + carried[d], new_state))
    return out


def _hg_kernel(q_ref, i_ref, zf_ref, zb_ref, g_ref, lb_ref, gn_ref, matf_ref, matb_ref, maskf_ref, maskb_ref,
               o_ref, yf_ref, yb_ref, *, n_chunks):
    c = CHUNK
    lb = lb_ref[0]

    def sweep(n, carry):
        s_f, s_b = carry
        sf = pl.ds(pl.multiple_of(n * c, c), c)
        sb = pl.ds(pl.multiple_of((n_chunks - 1 - n) * c, c), c)
        (yf_ref[sf, :], s_f), (yb_ref[sb, :], s_b) = _hg_pair(
            (q_ref[0, sf, :], i_ref[0, sf, :], zf_ref[0, sf, :].astype(F32), s_f),
            (q_ref[0, sb, :], i_ref[0, sb, :], zb_ref[0, sb, :].astype(F32), s_b),
            lb, (matf_ref, matb_ref), (maskf_ref, maskb_ref))
        return s_f, s_b

    zero = jnp.zeros((D_HEAD, D_HEAD), F32)
    lax.fori_loop(0, n_chunks, sweep, (zero, zero), unroll=4)

    def finish(n, carry):
        sl = pl.ds(pl.multiple_of(n * c, c), c)
        g = g_ref[0, sl, :].astype(F32)
        o_ref[0, sl, :] = (_rms(yf_ref[sl, :] + yb_ref[sl, :]) * gn_ref[0] * (g * _sigmoid(g))).astype(BF16)
        return carry

    lax.fori_loop(0, n_chunks, finish, 0, unroll=4)


def _hgrn2(u, lb, gn, consts):
    b, s, _ = u.shape
    mat_f, mat_b, mask_f, mask_b = consts

    def col(c0):
        return pl.BlockSpec((1, s, D_HEAD), lambda i, h: (i, 0, c0 + h))

    def whole(a):
        return pl.BlockSpec(a.shape, lambda i, h: (0,) * a.ndim)

    per_head = pl.BlockSpec((1, 1, D_HEAD), lambda i, h: (h, 0, 0))
    return pl.pallas_call(
        functools.partial(_hg_kernel, n_chunks=s // CHUNK),
        grid=(b, N_HEADS),
        in_specs=[col(COL_HGQ), col(COL_HGI), col(COL_HGFF), col(COL_HGFB), col(COL_HGG), per_head, per_head,
                  whole(mat_f), whole(mat_b), whole(mask_f), whole(mask_b)],
        out_specs=pl.BlockSpec((1, s, D_HEAD), lambda i, h: (i, 0, h)),
        out_shape=jax.ShapeDtypeStruct((b, s, N_HEADS * D_HEAD), BF16),
        scratch_shapes=[pltpu.VMEM((s, D_HEAD), F32), pltpu.VMEM((s, D_HEAD), F32)],
        compiler_params=pltpu.CompilerParams(dimension_semantics=("parallel", "parallel"),
                                             vmem_limit_bytes=VMEM_LIMIT),
        name="hgrn2",
    )(u, u, u, u, u, lb.reshape(N_HEADS, 1, D_HEAD), gn.reshape(N_HEADS, 1, D_HEAD),
      mat_f, mat_b, mask_f, mask_b)


def _merge_kernel(x_ref, yr_ref, yh_ref, za_ref, zb_ref, ga1_ref, sc2_ref, sh2_ref, gpost1_ref, gpre2_ref,
                  wr_ref, wh_ref, wo_ref, wrt_ref, x1_ref, h2_ref, aff_ref):
    br = jnp.dot(yr_ref[0], wr_ref[...], preferred_element_type=F32)
    bh = jnp.dot(yh_ref[0], wh_ref[...], preferred_element_type=F32)
    merged = _sigmoid(za_ref[0].astype(F32)) * br + _sigmoid(zb_ref[0].astype(F32)) * bh
    o = jnp.dot(merged.astype(BF16), wo_ref[...], preferred_element_type=F32)
    x1 = x_ref[0] + _rms(o) * (ga1_ref[0] * gpost1_ref[...])
    x1_ref[0] = x1
    h2 = _rms(x1) * (gpre2_ref[...] * (1.0 + sc2_ref[0])) + sh2_ref[0]
    h2_hi = h2.astype(BF16)
    h2_ref[0] = h2_hi
    h2_lo = (h2 - h2_hi.astype(F32)).astype(BF16)
    w_parts = wrt_ref[...]
    by_hi = lax.dot_general(w_parts, h2_hi, NT_DIMS, preferred_element_type=F32)
    logits = (by_hi[:N_EXPERTS] + by_hi[N_EXPERTS:]
              + lax.dot_general(w_parts[:N_EXPERTS], h2_lo, NT_DIMS, preferred_element_type=F32))
    ex = jnp.exp(logits - jnp.max(logits, axis=0, keepdims=True))
    aff_ref[...] = ex / jnp.sum(ex, axis=0, keepdims=True)


def _merge(x, y_ret, y_hg, u, ga1, sc2, sh2, g_post1, g_pre2, w_br_ret, w_br_hg, w_out, w_router_t, tm=512):
    b, s, d = x.shape
    nj = s // tm
    tok = lambda w: pl.BlockSpec((1, tm, w), lambda i, j: (i, j, 0))
    per_b = pl.BlockSpec((1, 1, d), lambda i, j: (i, 0, 0))
    vec = pl.BlockSpec((1, d), lambda i, j: (0, 0))
    whole = lambda a: pl.BlockSpec(a.shape, lambda i, j: (0,) * a.ndim)
    return pl.pallas_call(
        _merge_kernel,
        grid=(b, nj),
        in_specs=[tok(d), tok(N_HEADS * D_HEAD), tok(N_HEADS * D_HEAD),
                  pl.BlockSpec((1, tm, d), lambda i, j: (i, j, 0)),
                  pl.BlockSpec((1, tm, d), lambda i, j: (i, j, 1)),
                  per_b, per_b, per_b, vec, vec,
                  whole(w_br_ret), whole(w_br_hg), whole(w_out), whole(w_router_t)],
        out_specs=[tok(d), tok(d), pl.BlockSpec((N_EXPERTS, tm), lambda i, j: (0, i * nj + j))],
        out_shape=[jax.ShapeDtypeStruct((b, s, d), F32), jax.ShapeDtypeStruct((b, s, d), BF16),
                   jax.ShapeDtypeStruct((N_EXPERTS, b * s), F32)],
        compiler_params=pltpu.CompilerParams(dimension_semantics=("parallel", "parallel"),
                                             vmem_limit_bytes=VMEM_LIMIT),
        name="merge_outproj_router",
    )(x, y_ret, y_hg, u, u, ga1, sc2, sh2, g_post1.reshape(1, d), g_pre2.reshape(1, d),
      w_br_ret, w_br_hg, w_out, w_router_t)


TF_EXPERT = 512


def _expert_kernel(total_ref, x_ref, wg_ref, wu_ref, wd_ref, o_ref, *, tm):
    live_rows = total_ref[pl.program_id(0)] - pl.program_id(1) * tm
    half = tm // 2

    def swiglu(x):
        acc = jnp.zeros((x.shape[0], D_MODEL), F32)
        for f in range(D_EXPERT // TF_EXPERT):
            sl = slice(f * TF_EXPERT, (f + 1) * TF_EXPERT)
            a = jnp.dot(x, wg_ref[0, :, sl], preferred_element_type=F32)
            up = jnp.dot(x, wu_ref[0, :, sl], preferred_element_type=F32)
            acc += jnp.dot((a * _sigmoid(a) * up).astype(BF16), wd_ref[0, sl, :], preferred_element_type=F32)
        return acc.astype(BF16)

    @pl.when(live_rows > half)
    def _():
        o_ref[0] = swiglu(x_ref[0])

    @pl.when(jnp.logical_and(live_rows > 0, live_rows <= half))
    def _():
        o_ref[0, :half, :] = swiglu(x_ref[0, :half, :])
        o_ref[0, half:, :] = jnp.zeros((half, D_MODEL), BF16)

    @pl.when(live_rows <= 0)
    def _():
        o_ref[0] = jnp.zeros((tm, D_MODEL), BF16)


def _experts(total, xe, wg, wu, wd, tm=512):
    e, rows, d = xe.shape
    return pl.pallas_call(
        functools.partial(_expert_kernel, tm=tm),
        grid_spec=pltpu.PrefetchScalarGridSpec(
            num_scalar_prefetch=1,
            grid=(e, rows // tm),
            in_specs=[pl.BlockSpec((1, tm, d), lambda i, j, t: (i, j, 0)),
                      pl.BlockSpec((1, d, D_EXPERT), lambda i, j, t: (i, 0, 0)),
                      pl.BlockSpec((1, d, D_EXPERT), lambda i, j, t: (i, 0, 0)),
                      pl.BlockSpec((1, D_EXPERT, d), lambda i, j, t: (i, 0, 0))],
            out_specs=pl.BlockSpec((1, tm, d), lambda i, j, t: (i, j, 0))),
        out_shape=jax.ShapeDtypeStruct((e, rows, d), BF16),
        compiler_params=pltpu.CompilerParams(dimension_semantics=("parallel", "parallel"),
                                             vmem_limit_bytes=VMEM_LIMIT),
        name="expert_ffn",
    )(total, xe, wg, wu, wd)


TT = 512
WIN = 128
SEG_ALIGN = 16
NOT_CHOSEN = -float(1 << 20)


MIN_NORMAL = 2.0 ** -126
MANTISSA_STEPS = 53


def _threshold_kernel(a_ref, o_ref, *, cap):
    def count_ge(x):
        return jnp.sum(jnp.where(a_ref[...] >= x, 1.0, 0.0), axis=1, keepdims=True)

    lo = jnp.where(count_ge(MIN_NORMAL) >= cap, MIN_NORMAL, 0.0) * jnp.ones((N_EXPERTS, 1), F32)
    for j in (64, 32, 16, 8, 4, 2, 1):
        cand = lo * (2.0 ** j)
        lo = jnp.where(count_ge(cand) >= cap, cand, lo)
    hi = jnp.where(lo > 0.0, 2.0 * lo, MIN_NORMAL)

    def halve(_, bounds):
        lo, hi = bounds
        mid = 0.5 * (lo + hi)
        ok = count_ge(mid) >= cap
        return jnp.where(ok, mid, lo), jnp.where(ok, hi, mid)

    lo, hi = lax.fori_loop(0, MANTISSA_STEPS, halve, (lo, hi))
    a = a_ref[...]
    thr = jnp.min(jnp.where(a >= lo, a, 2.0), axis=1, keepdims=True)
    n_gt = jnp.sum(jnp.where(a > thr, 1.0, 0.0), axis=1, keepdims=True)
    o_ref[0] = jnp.broadcast_to(thr, (N_EXPERTS, LANES))
    o_ref[1] = jnp.broadcast_to(cap - n_gt, (N_EXPERTS, LANES))


def _thresholds(aff_t, cap):
    return pl.pallas_call(
        functools.partial(_threshold_kernel, cap=cap),
        out_shape=jax.ShapeDtypeStruct((2, N_EXPERTS, LANES), F32),
        compiler_params=pltpu.CompilerParams(vmem_limit_bytes=VMEM_LIMIT),
        name="route_threshold",
    )(aff_t)


def _select_kernel(a_ref, thr_ref, upper_ref, slot_t_ref, slot_ref, gate_ref, cnt_ref, seen_ref):
    @pl.when(pl.program_id(0) == 0)
    def _():
        seen_ref[...] = jnp.zeros_like(seen_ref)

    a = a_ref[...]
    thr = thr_ref[0][:, 0:1]
    need = thr_ref[1][:, 0:1]
    gt, eq = a > thr, a == thr
    marks = jnp.concatenate([jnp.where(gt, 1.0, 0.0), jnp.where(eq, 1.0, 0.0)], axis=0).astype(BF16)
    before = jnp.dot(marks, upper_ref[...], preferred_element_type=F32)
    gt_before, eq_before = before[:N_EXPERTS], before[N_EXPERTS:]
    seen = seen_ref[:, 0:1]
    eq_rank = seen + eq_before
    sel = jnp.logical_or(gt, jnp.logical_and(eq, eq_rank < need))
    rank = gt_before + jnp.minimum(eq_rank, need) - jnp.minimum(seen, need)
    slot_t = jnp.where(sel, rank, NOT_CHOSEN)
    slot_t_ref[...] = slot_t.astype(jnp.int32)
    cnt_ref[0] = jnp.broadcast_to(jnp.sum(jnp.where(sel, 1, 0), axis=1, keepdims=True), (N_EXPERTS, LANES))
    seen_ref[...] = seen_ref[...] + jnp.sum(jnp.where(eq, 1.0, 0.0), axis=1, keepdims=True)
    pad = jnp.zeros((LANES - N_EXPERTS, TT), F32)
    slot_ref[...] = jnp.concatenate([slot_t, pad], axis=0).T[:, :N_EXPERTS].astype(jnp.int32)
    gate_ref[...] = jnp.concatenate([a, pad], axis=0).T[:, :N_EXPERTS]


def _select(aff_t, thr):
    n_tok = aff_t.shape[1]
    n_tiles = n_tok // TT
    t = np.arange(TT)
    upper = jnp.asarray(t[:, None] < t[None, :], BF16)
    return pl.pallas_call(
        _select_kernel,
        grid=(n_tiles,),
        in_specs=[pl.BlockSpec((N_EXPERTS, TT), lambda i: (0, i)),
                  pl.BlockSpec((2, N_EXPERTS, LANES), lambda i: (0, 0, 0)),
                  pl.BlockSpec((TT, TT), lambda i: (0, 0))],
        out_specs=[pl.BlockSpec((N_EXPERTS, TT), lambda i: (0, i)),
                   pl.BlockSpec((TT, N_EXPERTS), lambda i: (i, 0)),
                   pl.BlockSpec((TT, N_EXPERTS), lambda i: (i, 0)),
                   pl.BlockSpec((1, N_EXPERTS, LANES), lambda i: (i, 0, 0))],
        out_shape=[jax.ShapeDtypeStruct((N_EXPERTS, n_tok), jnp.int32),
                   jax.ShapeDtypeStruct((n_tok, N_EXPERTS), jnp.int32),
                   jax.ShapeDtypeStruct((n_tok, N_EXPERTS), F32),
                   jax.ShapeDtypeStruct((n_tiles, N_EXPERTS, LANES), jnp.int32)],
        scratch_shapes=[pltpu.VMEM((N_EXPERTS, LANES), F32)],
        compiler_params=pltpu.CompilerParams(dimension_semantics=("arbitrary",)),
        name="route_select",
    )(aff_t, thr, upper)


def _dispatch_kernel(start_ref, off_ref, cnt_ref, slot_ref, h_ref, xe_in_ref, xe_ref,
                     stage_ref, extra_ref, carry_ref, sem, extra_sem):
    del xe_in_ref
    i = pl.program_id(0)
    par = i % 2
    tiles_per_win = WIN // SEG_ALIGN

    @pl.when(i == 0)
    def _():
        carry_ref[...] = jnp.zeros_like(carry_ref)

    row = lax.broadcasted_iota(jnp.int32, (WIN, TT), 0)
    slot = slot_ref[...]
    onehot = jnp.concatenate([jnp.where(slot[e:e + 1, :] + off_ref[i * N_EXPERTS + e] == row, 1.0, 0.0).astype(BF16)
                              for e in range(N_EXPERTS)], axis=0)
    stage_ref[par] = jnp.dot(onehot, h_ref[...], preferred_element_type=F32).astype(BF16)

    def window_copy(e):
        start = pl.multiple_of(start_ref[i * N_EXPERTS + e], SEG_ALIGN)
        return pltpu.make_async_copy(stage_ref.at[par, pl.ds(e * WIN, WIN), :],
                                     xe_ref.at[e, pl.ds(start, WIN), :], sem.at[e])

    for e in range(N_EXPERTS):
        end = off_ref[i * N_EXPERTS + e] + cnt_ref[i * N_EXPERTS + e]
        head = pl.ds(e * WIN, SEG_ALIGN)
        stage_ref[par, head, :] = stage_ref[par, head, :] + carry_ref[e]
        tile = jnp.minimum(end // SEG_ALIGN, tiles_per_win - 1)
        last = stage_ref[par, pl.ds(pl.multiple_of(e * WIN + tile * SEG_ALIGN, SEG_ALIGN), SEG_ALIGN), :]
        carry_ref[e] = jnp.where(end < WIN, last, jnp.zeros_like(last))

        @pl.when(i > 0)
        def _():
            window_copy(e).wait()

        window_copy(e).start()

    def extra_windows(e, carry):
        k = i * N_EXPERTS + e
        end = off_ref[k] + cnt_ref[k]

        def one(w, c):
            first = (w + 1) * WIN
            srow = slot_ref[pl.ds(e, 1), :] + (off_ref[k] - first)
            oh = jnp.where(srow == row, 1.0, 0.0).astype(BF16)
            extra_ref[...] = jnp.dot(oh, h_ref[...], preferred_element_type=F32).astype(BF16)
            cp = pltpu.make_async_copy(
                extra_ref, xe_ref.at[e, pl.ds(pl.multiple_of(start_ref[k] + first, SEG_ALIGN), WIN), :], extra_sem)
            cp.start()
            tile = end // SEG_ALIGN - (w + 1) * tiles_per_win
            inside = jnp.logical_and(tile >= 0, tile < tiles_per_win)
            last = extra_ref[pl.ds(pl.multiple_of(jnp.clip(tile, 0, tiles_per_win - 1) * SEG_ALIGN, SEG_ALIGN),
                                   SEG_ALIGN), :]
            carry_ref[e] = jnp.where(inside, last, carry_ref[e])
            cp.wait()
            return c

        return lax.fori_loop(0, jnp.maximum(end - 1, 0) // WIN, one, carry)

    lax.fori_loop(0, N_EXPERTS, extra_windows, 0)

    @pl.when(i == pl.num_programs(0) - 1)
    def _():
        for e in range(N_EXPERTS):
            window_copy(e).wait()


def _dispatch(start, off, cnt, slot_t, h2, xe0):
    n_tok, d = h2.shape
    rows = xe0.shape[1]
    return pl.pallas_call(
        _dispatch_kernel,
        grid_spec=pltpu.PrefetchScalarGridSpec(
            num_scalar_prefetch=3,
            grid=(n_tok // TT,),
            in_specs=[pl.BlockSpec((N_EXPERTS, TT), lambda i, *_: (0, i)),
                      pl.BlockSpec((TT, d), lambda i, *_: (i, 0)),
                      pl.BlockSpec(memory_space=pl.ANY)],
            out_specs=pl.BlockSpec(memory_space=pl.ANY),
            scratch_shapes=[pltpu.VMEM((2, N_EXPERTS * WIN, d), BF16), pltpu.VMEM((WIN, d), BF16),
                            pltpu.VMEM((N_EXPERTS, SEG_ALIGN, d), BF16),
                            pltpu.SemaphoreType.DMA((N_EXPERTS,)), pltpu.SemaphoreType.DMA(())]),
        out_shape=jax.ShapeDtypeStruct((N_EXPERTS, rows, d), BF16),
        input_output_aliases={5: 0},
        compiler_params=pltpu.CompilerParams(dimension_semantics=("arbitrary",), vmem_limit_bytes=VMEM_LIMIT),
        name="route_dispatch",
    )(start, off, cnt, slot_t, h2, xe0)


def _combine_kernel(start_ref, off_ref, cnt_ref, slot_ref, gate_ref, x1_ref, ga2_ref, g_ref, ye_ref, o_ref,
                    ybuf, extra_ref, acc_ref, sem, extra_sem):
    i = pl.program_id(0)
    par = i % 2

    def window_copy(step, e, buf):
        start = pl.multiple_of(start_ref[step * N_EXPERTS + e], SEG_ALIGN)
        return pltpu.make_async_copy(ye_ref.at[e, pl.ds(start, WIN), :],
                                     ybuf.at[buf, pl.ds(e * WIN, WIN), :], sem.at[buf])

    @pl.when(i == 0)
    def _():
        for e in range(N_EXPERTS):
            window_copy(0, e, 0).start()

    @pl.when(i + 1 < pl.num_programs(0))
    def _():
        for e in range(N_EXPERTS):
            window_copy(i + 1, e, 1 - par).start()

    slot, gate = slot_ref[...], gate_ref[...]
    lane = lax.broadcasted_iota(jnp.int32, (TT, WIN), 1)
    weights = jnp.concatenate([jnp.where(slot[:, e:e + 1] + off_ref[i * N_EXPERTS + e] == lane,
                                         gate[:, e:e + 1], 0.0).astype(BF16)
                               for e in range(N_EXPERTS)], axis=1)
    for e in range(N_EXPERTS):
        window_copy(i, e, par).wait()
    acc_ref[...] = jnp.dot(weights, ybuf[par], preferred_element_type=F32)

    def extra_windows(e, carry):
        k = i * N_EXPERTS + e

        def one(w, c):
            first = (w + 1) * WIN
            cp = pltpu.make_async_copy(
                ye_ref.at[e, pl.ds(pl.multiple_of(start_ref[k] + first, SEG_ALIGN), WIN), :], extra_ref, extra_sem)
            cp.start()
            pick = lax.broadcasted_iota(jnp.int32, (TT, N_EXPERTS), 1) == e
            slot_e = jnp.sum(jnp.where(pick, slot_ref[...], 0), axis=1, keepdims=True)
            gate_e = jnp.sum(jnp.where(pick, gate_ref[...], 0.0), axis=1, keepdims=True)
            wts = jnp.where(slot_e + (off_ref[k] - first) == lane, gate_e, 0.0).astype(BF16)
            cp.wait()
            acc_ref[...] += jnp.dot(wts, extra_ref[...], preferred_element_type=F32)
            return c

        return lax.fori_loop(0, jnp.maximum(off_ref[k] + cnt_ref[k] - 1, 0) // WIN, one, carry)

    lax.fori_loop(0, N_EXPERTS, extra_windows, 0)
    o_ref[...] = x1_ref[...] + ga2_ref[0] * (_rms(acc_ref[...]) * g_ref[...])


def _combine(start, off, cnt, slot, gate, x1, ga2, g_post2, ye, tiles_per_seq):
    n_tok, d = x1.shape
    tok = lambda w: pl.BlockSpec((TT, w), lambda i, *_: (i, 0))
    return pl.pallas_call(
        _combine_kernel,
        grid_spec=pltpu.PrefetchScalarGridSpec(
            num_scalar_prefetch=3,
            grid=(n_tok // TT,),
            in_specs=[tok(N_EXPERTS), tok(N_EXPERTS), tok(d),
                      pl.BlockSpec((1, 1, d), lambda i, *_: (i // tiles_per_seq, 0, 0)),
                      pl.BlockSpec((1, d), lambda i, *_: (0, 0)),
                      pl.BlockSpec(memory_space=pl.ANY)],
            out_specs=tok(d),
            scratch_shapes=[pltpu.VMEM((2, N_EXPERTS * WIN, d), BF16), pltpu.VMEM((WIN, d), BF16),
                            pltpu.VMEM((TT, d), F32),
                            pltpu.SemaphoreType.DMA((2,)), pltpu.SemaphoreType.DMA(())]),
        out_shape=jax.ShapeDtypeStruct((n_tok, d), F32),
        compiler_params=pltpu.CompilerParams(dimension_semantics=("arbitrary",), vmem_limit_bytes=VMEM_LIMIT),
        name="route_combine_final",
    )(start, off, cnt, slot, gate, x1, ga2, g_post2.reshape(1, d), ye)


def _moe(groups, g_post2, wg, wu, wd, tm_expert=512):
    d = D_MODEL
    routed = []
    used = 0
    for x1, h2, aff_t, ga2 in groups:
        b, s, _ = x1.shape
        n_tok = b * s
        cap = CAPACITY_FACTOR * n_tok // N_EXPERTS
        assert cap % SEG_ALIGN == 0
        slot_t, slot, gate, cnt = _select(aff_t, _thresholds(aff_t, cap))
        cnt = cnt[:, :, 0]
        base = jnp.cumsum(cnt, axis=0) - cnt + used
        start = base // SEG_ALIGN * SEG_ALIGN
        flat = lambda a: a.reshape(-1).astype(jnp.int32)
        routed.append((flat(start), flat(base - start), flat(cnt), slot_t, slot, gate))
        used += cap
    rows = -(-(used + WIN) // tm_expert) * tm_expert
    xe = jnp.zeros((N_EXPERTS, rows, d), BF16)
    for (x1, h2, _, _), (start, off, cnt, slot_t, _, _) in zip(groups, routed):
        xe = _dispatch(start, off, cnt, slot_t, h2.reshape(-1, d), xe)
    ye = _experts(jnp.full((N_EXPERTS,), used, jnp.int32), xe, wg, wu, wd, tm=tm_expert)
    outs = []
    for (x1, _, _, ga2), (start, off, cnt, _, slot, gate) in zip(groups, routed):
        b, s, _ = x1.shape
        outs.append(_combine(start, off, cnt, slot, gate, x1.reshape(-1, d), ga2, g_post2, ye, s // TT)
                    .reshape(b, s, d))
    return outs


def _hi_lo_rows(w):
    hi = w.astype(BF16)
    return jnp.concatenate([hi, (w - hi.astype(F32)).astype(BF16)], axis=0)


def _rope_tables(s):
    half = D_HEAD // 2
    inv = 1.0 / (ROPE_BASE ** (jnp.arange(half, dtype=F32) / half))
    ang = jnp.arange(s, dtype=F32)[:, None] * inv[None, :]
    cos, sin = jnp.cos(ang), jnp.sin(ang)
    return jnp.concatenate([cos, cos], axis=-1), jnp.concatenate([-sin, sin], axis=-1)


def _token_mixer(x, mod, p):
    b, s, d = x.shape
    sh1, sc1, ga1, sh2, sc2, ga2 = [m[:, None, :] for m in jnp.split(mod, 6, axis=-1)]
    cos2, sin2 = _rope_tables(s)
    u = _inproj(x, sc1, sh1, p["g_pre1"], cos2, sin2, p["w_in"])
    y_ret = _retention(u, p["log_gamma"], p["gn_ret"])
    y_hg = _hgrn2(u, p["lb"], p["gn_hg"], p["hg_consts"])
    x1, h2, aff_t = _merge(x, y_ret, y_hg, u, ga1, sc2, sh2, p["g_post1"], p["g_pre2"],
                           p["w_br_ret"], p["w_br_hg"], p["w_out"], p["w_router_t"])
    return x1, h2, aff_t, ga2


def kernel(x_prompt, x_sample, c_prompt, c_sample, w_ada, b_ada, g_pre1, g_post1, w_in, gn_ret, gn_hg, lb_logits, w_br_ret, w_br_hg, w_out, g_pre2, g_post2, w_router, w_gate, w_up, w_down):
    lower_bounds = jnp.cumsum(jax.nn.softmax(lb_logits.astype(F32), axis=0), axis=0)
    log_gamma = jnp.log(1.0 - jnp.power(2.0, -5.0 - jnp.arange(N_HEADS, dtype=F32)))
    consts = _hg_constants()
    n_gate = 2 * D_MODEL
    xp, xs = x_prompt, x_sample
    bp, bs = xp.shape[0], xs.shape[0]
    for l in range(w_ada.shape[0]):
        c_all = jnp.concatenate([c_prompt, c_sample, jnp.zeros((-(bp + bs) % 8, D_MODEL), F32)], axis=0)
        mod = _ada(c_all, w_ada[l], b_ada[l])
        w_in_l = w_in[l]
        p = dict(
            g_pre1=g_pre1[l], g_post1=g_post1[l], g_pre2=g_pre2[l], g_post2=g_post2[l],
            w_in=jnp.concatenate([w_in_l[:, -n_gate:], w_in_l[:, :-n_gate]], axis=1).astype(BF16),
            log_gamma=jnp.broadcast_to(log_gamma[:, None, None], (N_HEADS, 1, LANES)),
            gn_ret=gn_ret[l], gn_hg=gn_hg[l], lb=lower_bounds[l], hg_consts=consts,
            w_br_ret=w_br_ret[l].astype(BF16), w_br_hg=w_br_hg[l].astype(BF16), w_out=w_out[l].astype(BF16),
            w_router_t=_hi_lo_rows(w_router[l].T),
            w_gate=w_gate[l].astype(BF16), w_up=w_up[l].astype(BF16), w_down=w_down[l].astype(BF16),
        )
        mixed = [_token_mixer(xp, mod[:bp], p), _token_mixer(xs, mod[bp:bp + bs], p)]
        xp, xs = _moe(mixed, p["g_post2"], p["w_gate"], p["w_up"], p["w_down"])
    return (xp, xs)
```

```python
import functools

import numpy as np
import jax
import jax.numpy as jnp
from jax import lax
from jax.experimental import pallas as pl
from jax.experimental.pallas import tpu as pltpu

F32 = jnp.float32
BF16 = jnp.bfloat16

D_MODEL = 1024
N_HEADS = 4
D_HEAD = 128
N_EXPERTS = 16
CAPACITY_FACTOR = 2
D_EXPERT = 2048
ROPE_BASE = 10000.0
NORM_EPS = 1e-6
CHUNK = 128
LANES = 128
VMEM_LIMIT = 56 * 1024 * 1024

D_IN = 2 * D_MODEL + 9 * N_HEADS * D_HEAD
COL_RETQ, COL_RETK, COL_RETV, COL_RETG = 16, 20, 24, 28
COL_HGQ, COL_HGI, COL_HGFF, COL_HGFB, COL_HGG = 32, 36, 40, 44, 48
TN_IN = 512
ROPE_CHUNKS = (COL_RETQ * LANES // TN_IN, COL_RETK * LANES // TN_IN)

LOG2_E = 1.4426950408889634
NT_DIMS = (((1,), (1,)), ((), ()))
TN_DIMS = (((0,), (0,)), ((), ()))


def _sigmoid(x):
    return 1.0 / (1.0 + jnp.exp(-x))


def _rms(x, eps=NORM_EPS):
    return x * lax.rsqrt(jnp.mean(x * x, axis=-1, keepdims=True) + eps)


def _ada_kernel(c_ref, w_ref, b_ref, o_ref):
    c = c_ref[...]
    o_ref[...] = jnp.dot(c * _sigmoid(c), w_ref[...], preferred_element_type=F32,
                         precision=lax.Precision.HIGHEST) + b_ref[...]


def _ada(c_all, w_ada, b_ada):
    rows = c_all.shape[0]
    n_out = w_ada.shape[1]
    tn = 1024
    return pl.pallas_call(
        _ada_kernel,
        grid=(n_out // tn,),
        in_specs=[pl.BlockSpec((rows, D_MODEL), lambda n: (0, 0)),
                  pl.BlockSpec((D_MODEL, tn), lambda n: (0, n)),
                  pl.BlockSpec((1, tn), lambda n: (0, n))],
        out_specs=pl.BlockSpec((rows, tn), lambda n: (0, n)),
        out_shape=jax.ShapeDtypeStruct((rows, n_out), F32),
        name="ada_mod",
    )(c_all, w_ada, b_ada.reshape(1, n_out))


def _inproj_kernel(x_ref, sc_ref, sh_ref, g_ref, cos_ref, sin_ref, w_ref, u_ref):
    h = (_rms(x_ref[0]) * (g_ref[...] * (1.0 + sc_ref[0])) + sh_ref[0]).astype(BF16)
    for n in range(D_IN // TN_IN):
        acc = jnp.dot(h, w_ref[:, n * TN_IN:(n + 1) * TN_IN], preferred_element_type=F32)
        if n in ROPE_CHUNKS:
            scale = 1.0 if n == ROPE_CHUNKS[0] else D_HEAD ** -0.5
            for hd in range(TN_IN // D_HEAD):
                blk = acc[:, hd * D_HEAD:(hd + 1) * D_HEAD]
                rot = blk * cos_ref[...] + pltpu.roll(blk, D_HEAD // 2, axis=1) * sin_ref[...]
                lo = n * TN_IN + hd * D_HEAD
                u_ref[0, :, lo:lo + D_HEAD] = (rot * scale).astype(BF16)
        else:
            u_ref[0, :, n * TN_IN:(n + 1) * TN_IN] = acc.astype(BF16)


def _inproj(x, sc, sh, g, cos2, sin2, w_bf16, tm=512):
    b, s, d = x.shape
    return pl.pallas_call(
        _inproj_kernel,
        grid=(b, s // tm),
        in_specs=[pl.BlockSpec((1, tm, d), lambda i, j: (i, j, 0)),
                  pl.BlockSpec((1, 1, d), lambda i, j: (i, 0, 0)),
                  pl.BlockSpec((1, 1, d), lambda i, j: (i, 0, 0)),
                  pl.BlockSpec((1, d), lambda i, j: (0, 0)),
                  pl.BlockSpec((tm, D_HEAD), lambda i, j: (j, 0)),
                  pl.BlockSpec((tm, D_HEAD), lambda i, j: (j, 0)),
                  pl.BlockSpec((d, D_IN), lambda i, j: (0, 0))],
        out_specs=pl.BlockSpec((1, tm, D_IN), lambda i, j: (i, j, 0)),
        out_shape=jax.ShapeDtypeStruct((b, s, D_IN), BF16),
        compiler_params=pltpu.CompilerParams(dimension_semantics=("parallel", "parallel"),
                                             vmem_limit_bytes=VMEM_LIMIT),
        name="inproj",
    )(x, sc, sh, g.reshape(1, d), cos2, sin2, w_bf16)


def _ret_kernel(q_ref, k_ref, v_ref, g_ref, lg_ref, gn_ref, o_ref, yf_ref, yb_ref, *, n_chunks):
    c = CHUNK
    lg = lg_ref[0]
    pos = lax.broadcasted_iota(jnp.int32, (c, LANES), 0).astype(F32)
    dist = jnp.abs(lax.broadcasted_iota(jnp.int32, (c, c), 0)
                   - lax.broadcasted_iota(jnp.int32, (c, c), 1)).astype(F32)
    decay = jnp.exp(lg * dist)
    q_fwd = jnp.exp(lg * (pos + 1.0)).astype(BF16)
    k_fwd = jnp.exp(lg * (c - 1.0 - pos)).astype(BF16)
    q_bwd = jnp.exp(lg * (c - pos)).astype(BF16)
    k_bwd = jnp.exp(lg * pos).astype(BF16)
    g_chunk = jnp.exp(lg * c)

    def scaled(t, f):
        return t * f

    def sweep(n, carry):
        r_f, r_b = carry
        sl = pl.ds(pl.multiple_of(n * c, c), c)
        q, k, v = q_ref[0, sl, :], k_ref[0, sl, :], v_ref[0, sl, :]
        s = lax.dot_general(q, k, NT_DIMS, preferred_element_type=F32) * decay
        lhs = jnp.concatenate([s.astype(BF16), scaled(q, q_fwd)], axis=1)
        rhs = jnp.concatenate([v, r_f.astype(BF16)], axis=0)
        yf_ref[sl, :] = jnp.dot(lhs, rhs, preferred_element_type=F32)
        r_f = g_chunk * r_f + lax.dot_general(scaled(k, k_fwd), v, TN_DIMS, preferred_element_type=F32)
        sl = pl.ds(pl.multiple_of((n_chunks - 1 - n) * c, c), c)
        q, k, v = q_ref[0, sl, :], k_ref[0, sl, :], v_ref[0, sl, :]
        yb_ref[sl, :] = jnp.dot(scaled(q, q_bwd), r_b.astype(BF16), preferred_element_type=F32)
        r_b = g_chunk * r_b + lax.dot_general(scaled(k, k_bwd), v, TN_DIMS, preferred_element_type=F32)
        return r_f, r_b

    zero = jnp.zeros((D_HEAD, D_HEAD), F32)
    lax.fori_loop(0, n_chunks, sweep, (zero, zero), unroll=4)

    def finish(n, carry):
        sl = pl.ds(pl.multiple_of(n * c, c), c)
        y = yf_ref[sl, :] + yb_ref[sl, :]
        yc = y - jnp.mean(y, axis=-1, keepdims=True)
        g = g_ref[0, sl, :].astype(F32)
        o_ref[0, sl, :] = (_rms(yc) * gn_ref[0] * (g * _sigmoid(g))).astype(BF16)
        return carry

    lax.fori_loop(0, n_chunks, finish, 0, unroll=4)


def _retention(u, log_gamma, gn):
    b, s, _ = u.shape

    def col(c0):
        return pl.BlockSpec((1, s, D_HEAD), lambda i, h: (i, 0, c0 + h))

    per_head = pl.BlockSpec((1, 1, D_HEAD), lambda i, h: (h, 0, 0))
    return pl.pallas_call(
        functools.partial(_ret_kernel, n_chunks=s // CHUNK),
        grid=(b, N_HEADS),
        in_specs=[col(COL_RETQ), col(COL_RETK), col(COL_RETV), col(COL_RETG), per_head, per_head],
        out_specs=pl.BlockSpec((1, s, D_HEAD), lambda i, h: (i, 0, h)),
        out_shape=jax.ShapeDtypeStruct((b, s, N_HEADS * D_HEAD), BF16),
        scratch_shapes=[pltpu.VMEM((s, D_HEAD), F32), pltpu.VMEM((s, D_HEAD), F32)],
        compiler_params=pltpu.CompilerParams(dimension_semantics=("parallel", "parallel"),
                                             vmem_limit_bytes=VMEM_LIMIT),
        name="retention",
    )(u, u, u, u, log_gamma, gn.reshape(N_HEADS, 1, D_HEAD))


N_LEVELS = 8
ROW_PIECE = 16
N_SPLIT_LEVEL = 5


def _hg_constants():
    c = CHUNK
    p = np.arange(c)
    tril = (p[None, :] <= p[:, None]).astype(np.float64)
    masks = [np.eye(c)]
    for l in range(1, N_LEVELS):
        s, half = 2 ** l, 2 ** (l - 1)
        blk = p // s
        upper = (p % s) >= half
        masks.append(((blk[:, None] == blk[None, :]) & upper[:, None] & (~upper)[None, :]).astype(np.float64))
    mask_f = np.stack(masks)
    mask_b = np.stack([m.T for m in masks])
    tile2 = lambda m: np.concatenate([m, m], axis=1)
    return (jnp.asarray(tile2(tril), BF16), jnp.asarray(tile2(tril[::-1, ::-1]), BF16),
            jnp.asarray(mask_f, BF16), jnp.asarray(mask_b, BF16))


def _neg_block_offsets(cum, level, forward):
    s = 2 ** level
    half = s // 2
    blocks = cum.reshape(CHUNK // s, s, LANES)
    lower, upper = blocks[:, :half, :], blocks[:, half:, :]
    if forward:
        ref = blocks[:, half - 1:half, :]
        out = jnp.concatenate([ref - lower, upper - ref], axis=1)
    else:
        ref = blocks[:, half:half + 1, :]
        out = jnp.concatenate([lower - ref, ref - upper], axis=1)
    return out.reshape(CHUNK, LANES)


SUBLANES = 8


def _neg_small_offsets(cum, lf, level, forward):
    s = 2 ** level
    half = s // 2
    row = lax.broadcasted_iota(jnp.int32, (CHUNK // SUBLANES, SUBLANES, LANES), 1)
    in_lower = (row & half) == 0
    if level == 1:
        keep = jnp.logical_not(in_lower) if forward else in_lower
        return jnp.where(keep.reshape(CHUNK, LANES), lf, 0.0)
    rows = cum.reshape(CHUNK // SUBLANES, SUBLANES, LANES)
    first = half - 1 if forward else half
    ref = rows[:, first:first + 1, :]
    for blk in range(1, SUBLANES // s):
        r = blk * s + first
        ref = jnp.where(row < blk * s, ref, rows[:, r:r + 1, :])
    diff = rows - ref
    return jnp.where(in_lower == forward, -diff, diff).reshape(CHUNK, LANES)


def _hg_gates(z, lb, mat_ref):
    sg = _sigmoid(z)
    lf = jnp.log(lb + (1.0 - lb) * sg) * LOG2_E
    kb = ((1.0 - lb) * (1.0 - sg)).astype(BF16)
    hi = lf.astype(BF16)
    mid = (lf - hi.astype(F32)).astype(BF16)
    return kb, lf, jnp.dot(mat_ref[...], jnp.concatenate([hi, mid], axis=0), preferred_element_type=F32)


def _block_diag(a, b):
    zero = jnp.zeros_like(a)
    return jnp.concatenate([jnp.concatenate([a, zero], axis=1), jnp.concatenate([zero, b], axis=1)], axis=0)


def _hg_pair(fwd, bwd, lb, mats, masks):
    c = CHUNK
    dirs = (True, False)
    q = [fwd[0], bwd[0]]
    v = [fwd[1], bwd[1]]
    state = [fwd[3], bwd[3]]
    kb, lf, cum = zip(*[_hg_gates(d[2], lb, m) for d, m in zip((fwd, bwd), mats)])
    piece = lambda t, i: t[ROW_PIECE * i:ROW_PIECE * (i + 1)]
    n_pieces = c // ROW_PIECE
    halves = lambda p: (p[:, :c], p[:, c:])

    def paired_nt(lhs, rhs):
        return halves(lax.dot_general(jnp.concatenate(lhs, axis=1), _block_diag(*rhs), NT_DIMS,
                                      preferred_element_type=F32))

    def scale(l):
        neg = [_neg_small_offsets(cum[d], lf[d], l, dirs[d]) if 2 ** l <= SUBLANES
               else _neg_block_offsets(cum[d], l, dirs[d]) for d in range(2)]
        return [jnp.exp2(n).astype(BF16) for n in neg]

    a = [p.astype(BF16) * m[0] for p, m in zip(paired_nt(q, kb), masks)]
    for l in range(1, N_SPLIT_LEVEL):
        e = scale(l)
        p = paired_nt([q[d] * e[d] for d in range(2)], [kb[d] * e[d] for d in range(2)])
        a = [a[d] + p[d].astype(BF16) * masks[d][l] for d in range(2)]
    a = [[piece(a[d], i) for i in range(n_pieces)] for d in range(2)]
    for l in range(N_SPLIT_LEVEL, N_LEVELS):
        e = scale(l)
        per_half = 2 ** (l - 1) // ROW_PIECE
        is_query = [[((i // per_half) % 2 == 1) == dirs[d] for i in range(n_pieces)] for d in range(2)]
        queries = [[i for i in range(n_pieces) if is_query[d][i]] for d in range(2)]
        q_rows = [jnp.concatenate([piece(q[d], i) * piece(e[d], i) for i in queries[d]], axis=0) for d in range(2)]
        k_rows = [jnp.concatenate([piece(kb[d], i) if is_query[d][i] else piece(kb[d], i) * piece(e[d], i)
                                   for i in range(n_pieces)], axis=0) for d in range(2)]
        p = paired_nt(q_rows, k_rows)
        for d in range(2):
            for n, i in enumerate(queries[d]):
                a[d][i] = a[d][i] + (piece(p[d], n).astype(BF16)
                                     * masks[d][l, ROW_PIECE * i:ROW_PIECE * (i + 1), :])
    a = [jnp.concatenate(a[d], axis=0) for d in range(2)]
    y = halves(jnp.dot(jnp.concatenate(a, axis=1), _block_diag(*v), preferred_element_type=F32))
    carried = paired_nt([q[d] * jnp.exp2(cum[d]).astype(BF16) for d in range(2)],
                        [state[d].astype(BF16) for d in range(2)])
    out = []
    for d in range(2):
        total = cum[d][c - 1:c] if dirs[d] else cum[d][0:1]
        kd = kb[d] * jnp.exp2(total - cum[d]).astype(BF16)
        new_state = state[d] * jnp.exp2(total) + lax.dot_general(v[d], kd, TN_DIMS, preferred_element_type=F32)
        out.append((y[d] + carried[d], new_state))
    return out


def _hg_kernel(q_ref, i_ref, zf_ref, zb_ref, g_ref, lb_ref, gn_ref, matf_ref, matb_ref, maskf_ref, maskb_ref,
               o_ref, yf_ref, yb_ref, *, n_chunks):
    c = CHUNK
    lb = lb_ref[0]

    def sweep(n, carry):
        s_f, s_b = carry
        sf = pl.ds(pl.multiple_of(n * c, c), c)
        sb = pl.ds(pl.multiple_of((n_chunks - 1 - n) * c, c), c)
        (yf_ref[sf, :], s_f), (yb_ref[sb, :], s_b) = _hg_pair(
            (q_ref[0, sf, :], i_ref[0, sf, :], zf_ref[0, sf, :].astype(F32), s_f),
            (q_ref[0, sb, :], i_ref[0, sb, :], zb_ref[0, sb, :].astype(F32), s_b),
            lb, (matf_ref, matb_ref), (maskf_ref, maskb_ref))
        return s_f, s_b

    zero = jnp.zeros((D_HEAD, D_HEAD), F32)
    lax.fori_loop(0, n_chunks, sweep, (zero, zero), unroll=4)

    def finish(n, carry):
        sl = pl.ds(pl.multiple_of(n * c, c), c)
        g = g_ref[0, sl, :].astype(F32)
        o_ref[0, sl, :] = (_rms(yf_ref[sl, :] + yb_ref[sl, :]) * gn_ref[0] * (g * _sigmoid(g))).astype(BF16)
        return carry

    lax.fori_loop(0, n_chunks, finish, 0, unroll=4)


def _hgrn2(u, lb, gn, consts):
    b, s, _ = u.shape
    mat_f, mat_b, mask_f, mask_b = consts

    def col(c0):
        return pl.BlockSpec((1, s, D_HEAD), lambda i, h: (i, 0, c0 + h))

    def whole(a):
        return pl.BlockSpec(a.shape, lambda i, h: (0,) * a.ndim)

    per_head = pl.BlockSpec((1, 1, D_HEAD), lambda i, h: (h, 0, 0))
    return pl.pallas_call(
        functools.partial(_hg_kernel, n_chunks=s // CHUNK),
        grid=(b, N_HEADS),
        in_specs=[col(COL_HGQ), col(COL_HGI), col(COL_HGFF), col(COL_HGFB), col(COL_HGG), per_head, per_head,
                  whole(mat_f), whole(mat_b), whole(mask_f), whole(mask_b)],
        out_specs=pl.BlockSpec((1, s, D_HEAD), lambda i, h: (i, 0, h)),
        out_shape=jax.ShapeDtypeStruct((b, s, N_HEADS * D_HEAD), BF16),
        scratch_shapes=[pltpu.VMEM((s, D_HEAD), F32), pltpu.VMEM((s, D_HEAD), F32)],
        compiler_params=pltpu.CompilerParams(dimension_semantics=("parallel", "parallel"),
                                             vmem_limit_bytes=VMEM_LIMIT),
        name="hgrn2",
    )(u, u, u, u, u, lb.reshape(N_HEADS, 1, D_HEAD), gn.reshape(N_HEADS, 1, D_HEAD),
      mat_f, mat_b, mask_f, mask_b)


def _merge_kernel(x_ref, yr_ref, yh_ref, za_ref, zb_ref, ga1_ref, sc2_ref, sh2_ref, gpost1_ref, gpre2_ref,
                  wr_ref, wh_ref, wo_ref, wrt_ref, x1_ref, h2_ref, aff_ref):
    br = jnp.dot(yr_ref[0], wr_ref[...], preferred_element_type=F32)
    bh = jnp.dot(yh_ref[0], wh_ref[...], preferred_element_type=F32)
    merged = _sigmoid(za_ref[0].astype(F32)) * br + _sigmoid(zb_ref[0].astype(F32)) * bh
    o = jnp.dot(merged.astype(BF16), wo_ref[...], preferred_element_type=F32)
    x1 = x_ref[0] + _rms(o) * (ga1_ref[0] * gpost1_ref[...])
    x1_ref[0] = x1
    h2 = _rms(x1) * (gpre2_ref[...] * (1.0 + sc2_ref[0])) + sh2_ref[0]
    h2_hi = h2.astype(BF16)
    h2_ref[0] = h2_hi
    h2_lo = (h2 - h2_hi.astype(F32)).astype(BF16)
    w_parts = wrt_ref[...]
    by_hi = lax.dot_general(w_parts, h2_hi, NT_DIMS, preferred_element_type=F32)
    logits = (by_hi[:N_EXPERTS] + by_hi[N_EXPERTS:]
              + lax.dot_general(w_parts[:N_EXPERTS], h2_lo, NT_DIMS, preferred_element_type=F32))
    ex = jnp.exp(logits - jnp.max(logits, axis=0, keepdims=True))
    aff_ref[...] = ex / jnp.sum(ex, axis=0, keepdims=True)


def _merge(x, y_ret, y_hg, u, ga1, sc2, sh2, g_post1, g_pre2, w_br_ret, w_br_hg, w_out, w_router_t, tm=512):
    b, s, d = x.shape
    nj = s // tm
    tok = lambda w: pl.BlockSpec((1, tm, w), lambda i, j: (i, j, 0))
    per_b = pl.BlockSpec((1, 1, d), lambda i, j: (i, 0, 0))
    vec = pl.BlockSpec((1, d), lambda i, j: (0, 0))
    whole = lambda a: pl.BlockSpec(a.shape, lambda i, j: (0,) * a.ndim)
    return pl.pallas_call(
        _merge_kernel,
        grid=(b, nj),
        in_specs=[tok(d), tok(N_HEADS * D_HEAD), tok(N_HEADS * D_HEAD),
                  pl.BlockSpec((1, tm, d), lambda i, j: (i, j, 0)),
                  pl.BlockSpec((1, tm, d), lambda i, j: (i, j, 1)),
                  per_b, per_b, per_b, vec, vec,
                  whole(w_br_ret), whole(w_br_hg), whole(w_out), whole(w_router_t)],
        out_specs=[tok(d), tok(d), pl.BlockSpec((N_EXPERTS, tm), lambda i, j: (0, i * nj + j))],
        out_shape=[jax.ShapeDtypeStruct((b, s, d), F32), jax.ShapeDtypeStruct((b, s, d), BF16),
                   jax.ShapeDtypeStruct((N_EXPERTS, b * s), F32)],
        compiler_params=pltpu.CompilerParams(dimension_semantics=("parallel", "parallel"),
                                             vmem_limit_bytes=VMEM_LIMIT),
        name="merge_outproj_router",
    )(x, y_ret, y_hg, u, u, ga1, sc2, sh2, g_post1.reshape(1, d), g_pre2.reshape(1, d),
      w_br_ret, w_br_hg, w_out, w_router_t)


TF_EXPERT = 512


N_FF_CHUNKS = D_EXPERT // TF_EXPERT


def _expert_kernel(total_ref, x_ref, wg_ref, wu_ref, wd_ref, o_ref, wg_bf, wu_bf, wd_bf, *, tm, n_tiles):
    s, j = pl.program_id(0), pl.program_id(1)
    fill, use = s % 2, 1 - s % 2
    for c in range(3 * N_FF_CHUNKS):
        @pl.when(jnp.logical_and(j == c, s < pl.num_programs(0) - 1))
        def _():
            which, f = divmod(c, N_FF_CHUNKS)
            src, dst = ((wg_ref, wg_bf), (wu_ref, wu_bf), (wd_ref, wd_bf))[which]
            dst[fill, f] = src[0].astype(BF16)

    live_rows = jnp.where(jnp.logical_and(s > 0, j < n_tiles), total_ref[jnp.maximum(s - 1, 0)] - j * tm, 0)
    half = tm // 2

    def swiglu(x):
        acc = jnp.zeros((x.shape[0], D_MODEL), F32)
        for f in range(N_FF_CHUNKS):
            a = jnp.dot(x, wg_bf[use, f], preferred_element_type=F32)
            up = jnp.dot(x, wu_bf[use, f], preferred_element_type=F32)
            acc += jnp.dot((a * _sigmoid(a) * up).astype(BF16), wd_bf[use, f], preferred_element_type=F32)
        return acc.astype(BF16)

    @pl.when(live_rows > half)
    def _():
        o_ref[0] = swiglu(x_ref[0])

    @pl.when(jnp.logical_and(live_rows > 0, live_rows <= half))
    def _():
        o_ref[0, :half, :] = swiglu(x_ref[0, :half, :])
        o_ref[0, half:, :] = jnp.zeros((half, D_MODEL), BF16)

    @pl.when(jnp.logical_and(live_rows <= 0, j < n_tiles))
    def _():
        o_ref[0] = jnp.zeros((tm, D_MODEL), BF16)


def _experts(total, xe, wg, wu, wd, tm=512):
    e, rows, d = xe.shape
    n_tiles = rows // tm
    steps = max(n_tiles, 3 * N_FF_CHUNKS)
    nxt = lambda s: jnp.minimum(s, e - 1)
    cur = lambda s: jnp.maximum(s - 1, 0)
    tile = lambda j: jnp.minimum(j, n_tiles - 1)
    chunk = lambda j, which: jnp.clip(j - which * N_FF_CHUNKS, 0, N_FF_CHUNKS - 1)
    return pl.pallas_call(
        functools.partial(_expert_kernel, tm=tm, n_tiles=n_tiles),
        grid_spec=pltpu.PrefetchScalarGridSpec(
            num_scalar_prefetch=1,
            grid=(e + 1, steps),
            in_specs=[pl.BlockSpec((1, tm, d), lambda s, j, t: (cur(s), tile(j), 0)),
                      pl.BlockSpec((1, d, TF_EXPERT), lambda s, j, t: (nxt(s), 0, chunk(j, 0))),
                      pl.BlockSpec((1, d, TF_EXPERT), lambda s, j, t: (nxt(s), 0, chunk(j, 1))),
                      pl.BlockSpec((1, TF_EXPERT, d), lambda s, j, t: (nxt(s), chunk(j, 2), 0))],
            out_specs=pl.BlockSpec((1, tm, d), lambda s, j, t: (jnp.where(s == 0, e, s - 1), tile(j), 0)),
            scratch_shapes=[pltpu.VMEM((2, N_FF_CHUNKS, d, TF_EXPERT), BF16),
                            pltpu.VMEM((2, N_FF_CHUNKS, d, TF_EXPERT), BF16),
                            pltpu.VMEM((2, N_FF_CHUNKS, TF_EXPERT, d), BF16)]),
        out_shape=jax.ShapeDtypeStruct((e + 1, rows, d), BF16),
        compiler_params=pltpu.CompilerParams(dimension_semantics=("arbitrary", "arbitrary"),
                                             vmem_limit_bytes=VMEM_LIMIT),
        name="expert_ffn",
    )(total, xe, wg, wu, wd)


TT = 512
WIN = 128
SEG_ALIGN = 16
NOT_CHOSEN = -float(1 << 20)


MIN_NORMAL = 2.0 ** -126
MANTISSA_STEPS = 53


def _threshold_kernel(a_ref, o_ref, *, cap):
    def count_ge(x):
        return jnp.sum(jnp.where(a_ref[...] >= x, 1.0, 0.0), axis=1, keepdims=True)

    lo = jnp.where(count_ge(MIN_NORMAL) >= cap, MIN_NORMAL, 0.0) * jnp.ones((N_EXPERTS, 1), F32)
    for j in (64, 32, 16, 8, 4, 2, 1):
        cand = lo * (2.0 ** j)
        lo = jnp.where(count_ge(cand) >= cap, cand, lo)
    hi = jnp.where(lo > 0.0, 2.0 * lo, MIN_NORMAL)

    def halve(_, bounds):
        lo, hi = bounds
        mid = 0.5 * (lo + hi)
        ok = count_ge(mid) >= cap
        return jnp.where(ok, mid, lo), jnp.where(ok, hi, mid)

    lo, hi = lax.fori_loop(0, MANTISSA_STEPS, halve, (lo, hi))
    a = a_ref[...]
    thr = jnp.min(jnp.where(a >= lo, a, 2.0), axis=1, keepdims=True)
    n_gt = jnp.sum(jnp.where(a > thr, 1.0, 0.0), axis=1, keepdims=True)
    o_ref[0] = jnp.broadcast_to(thr, (N_EXPERTS, LANES))
    o_ref[1] = jnp.broadcast_to(cap - n_gt, (N_EXPERTS, LANES))


def _thresholds(aff_t, cap):
    return pl.pallas_call(
        functools.partial(_threshold_kernel, cap=cap),
        out_shape=jax.ShapeDtypeStruct((2, N_EXPERTS, LANES), F32),
        compiler_params=pltpu.CompilerParams(vmem_limit_bytes=VMEM_LIMIT),
        name="route_threshold",
    )(aff_t)


def _select_kernel(a_ref, thr_ref, upper_ref, slot_t_ref, slot_ref, gate_ref, cnt_ref, seen_ref):
    @pl.when(pl.program_id(0) == 0)
    def _():
        seen_ref[...] = jnp.zeros_like(seen_ref)

    a = a_ref[...]
    thr = thr_ref[0][:, 0:1]
    need = thr_ref[1][:, 0:1]
    gt, eq = a > thr, a == thr
    marks = jnp.concatenate([jnp.where(gt, 1.0, 0.0), jnp.where(eq, 1.0, 0.0)], axis=0).astype(BF16)
    before = jnp.dot(marks, upper_ref[...], preferred_element_type=F32)
    gt_before, eq_before = before[:N_EXPERTS], before[N_EXPERTS:]
    seen = seen_ref[:, 0:1]
    eq_rank = seen + eq_before
    sel = jnp.logical_or(gt, jnp.logical_and(eq, eq_rank < need))
    rank = gt_before + jnp.minimum(eq_rank, need) - jnp.minimum(seen, need)
    slot_t = jnp.where(sel, rank, NOT_CHOSEN)
    slot_t_ref[...] = slot_t.astype(jnp.int32)
    cnt_ref[0] = jnp.broadcast_to(jnp.sum(jnp.where(sel, 1, 0), axis=1, keepdims=True), (N_EXPERTS, LANES))
    seen_ref[...] = seen_ref[...] + jnp.sum(jnp.where(eq, 1.0, 0.0), axis=1, keepdims=True)
    pad = jnp.zeros((LANES - N_EXPERTS, TT), F32)
    slot_ref[...] = jnp.concatenate([slot_t, pad], axis=0).T[:, :N_EXPERTS].astype(jnp.int32)
    gate_ref[...] = jnp.concatenate([a, pad], axis=0).T[:, :N_EXPERTS]


def _select(aff_t, thr):
    n_tok = aff_t.shape[1]
    n_tiles = n_tok // TT
    t = np.arange(TT)
    upper = jnp.asarray(t[:, None] < t[None, :], BF16)
    return pl.pallas_call(
        _select_kernel,
        grid=(n_tiles,),
        in_specs=[pl.BlockSpec((N_EXPERTS, TT), lambda i: (0, i)),
                  pl.BlockSpec((2, N_EXPERTS, LANES), lambda i: (0, 0, 0)),
                  pl.BlockSpec((TT, TT), lambda i: (0, 0))],
        out_specs=[pl.BlockSpec((N_EXPERTS, TT), lambda i: (0, i)),
                   pl.BlockSpec((TT, N_EXPERTS), lambda i: (i, 0)),
                   pl.BlockSpec((TT, N_EXPERTS), lambda i: (i, 0)),
                   pl.BlockSpec((1, N_EXPERTS, LANES), lambda i: (i, 0, 0))],
        out_shape=[jax.ShapeDtypeStruct((N_EXPERTS, n_tok), jnp.int32),
                   jax.ShapeDtypeStruct((n_tok, N_EXPERTS), jnp.int32),
                   jax.ShapeDtypeStruct((n_tok, N_EXPERTS), F32),
                   jax.ShapeDtypeStruct((n_tiles, N_EXPERTS, LANES), jnp.int32)],
        scratch_shapes=[pltpu.VMEM((N_EXPERTS, LANES), F32)],
        compiler_params=pltpu.CompilerParams(dimension_semantics=("arbitrary",)),
        name="route_select",
    )(aff_t, thr, upper)


def _dispatch_kernel(start_ref, off_ref, cnt_ref, slot_ref, h_ref, xe_in_ref, xe_ref,
                     stage_ref, extra_ref, carry_ref, sem, extra_sem):
    del xe_in_ref
    i = pl.program_id(0)
    par = i % 2
    tiles_per_win = WIN // SEG_ALIGN

    @pl.when(i == 0)
    def _():
        carry_ref[...] = jnp.zeros_like(carry_ref)

    row = lax.broadcasted_iota(jnp.int32, (WIN, TT), 0)
    slot = slot_ref[...]
    onehot = jnp.concatenate([jnp.where(slot[e:e + 1, :] + off_ref[i * N_EXPERTS + e] == row, 1.0, 0.0).astype(BF16)
                              for e in range(N_EXPERTS)], axis=0)
    stage_ref[par] = jnp.dot(onehot, h_ref[...], preferred_element_type=F32).astype(BF16)

    def window_copy(e):
        start = pl.multiple_of(start_ref[i * N_EXPERTS + e], SEG_ALIGN)
        return pltpu.make_async_copy(stage_ref.at[par, pl.ds(e * WIN, WIN), :],
                                     xe_ref.at[e, pl.ds(start, WIN), :], sem.at[e])

    for e in range(N_EXPERTS):
        end = off_ref[i * N_EXPERTS + e] + cnt_ref[i * N_EXPERTS + e]
        head = pl.ds(e * WIN, SEG_ALIGN)
        stage_ref[par, head, :] = stage_ref[par, head, :] + carry_ref[e]
        tile = jnp.minimum(end // SEG_ALIGN, tiles_per_win - 1)
        last = stage_ref[par, pl.ds(pl.multiple_of(e * WIN + tile * SEG_ALIGN, SEG_ALIGN), SEG_ALIGN), :]
        carry_ref[e] = jnp.where(end < WIN, last, jnp.zeros_like(last))

        @pl.when(i > 0)
        def _():
            window_copy(e).wait()

        window_copy(e).start()

    def extra_windows(e, carry):
        k = i * N_EXPERTS + e
        end = off_ref[k] + cnt_ref[k]

        def one(w, c):
            first = (w + 1) * WIN
            srow = slot_ref[pl.ds(e, 1), :] + (off_ref[k] - first)
            oh = jnp.where(srow == row, 1.0, 0.0).astype(BF16)
            extra_ref[...] = jnp.dot(oh, h_ref[...], preferred_element_type=F32).astype(BF16)
            cp = pltpu.make_async_copy(
                extra_ref, xe_ref.at[e, pl.ds(pl.multiple_of(start_ref[k] + first, SEG_ALIGN), WIN), :], extra_sem)
            cp.start()
            tile = end // SEG_ALIGN - (w + 1) * tiles_per_win
            inside = jnp.logical_and(tile >= 0, tile < tiles_per_win)
            last = extra_ref[pl.ds(pl.multiple_of(jnp.clip(tile, 0, tiles_per_win - 1) * SEG_ALIGN, SEG_ALIGN),
                                   SEG_ALIGN), :]
            carry_ref[e] = jnp.where(inside, last, carry_ref[e])
            cp.wait()
            return c

        return lax.fori_loop(0, jnp.maximum(end - 1, 0) // WIN, one, carry)

    lax.fori_loop(0, N_EXPERTS, extra_windows, 0)

    @pl.when(i == pl.num_programs(0) - 1)
    def _():
        for e in range(N_EXPERTS):
            window_copy(e).wait()


def _dispatch(start, off, cnt, slot_t, h2, xe0):
    n_tok, d = h2.shape
    rows = xe0.shape[1]
    return pl.pallas_call(
        _dispatch_kernel,
        grid_spec=pltpu.PrefetchScalarGridSpec(
            num_scalar_prefetch=3,
            grid=(n_tok // TT,),
            in_specs=[pl.BlockSpec((N_EXPERTS, TT), lambda i, *_: (0, i)),
                      pl.BlockSpec((TT, d), lambda i, *_: (i, 0)),
                      pl.BlockSpec(memory_space=pl.ANY)],
            out_specs=pl.BlockSpec(memory_space=pl.ANY),
            scratch_shapes=[pltpu.VMEM((2, N_EXPERTS * WIN, d), BF16), pltpu.VMEM((WIN, d), BF16),
                            pltpu.VMEM((N_EXPERTS, SEG_ALIGN, d), BF16),
                            pltpu.SemaphoreType.DMA((N_EXPERTS,)), pltpu.SemaphoreType.DMA(())]),
        out_shape=jax.ShapeDtypeStruct((N_EXPERTS, rows, d), BF16),
        input_output_aliases={5: 0},
        compiler_params=pltpu.CompilerParams(dimension_semantics=("arbitrary",), vmem_limit_bytes=VMEM_LIMIT),
        name="route_dispatch",
    )(start, off, cnt, slot_t, h2, xe0)


def _combine_kernel(start_ref, off_ref, cnt_ref, slot_ref, gate_ref, x1_ref, ga2_ref, g_ref, ye_ref, o_ref,
                    ybuf, extra_ref, acc_ref, sem, extra_sem):
    i = pl.program_id(0)
    par = i % 2

    def window_copy(step, e, buf):
        start = pl.multiple_of(start_ref[step * N_EXPERTS + e], SEG_ALIGN)
        return pltpu.make_async_copy(ye_ref.at[e, pl.ds(start, WIN), :],
                                     ybuf.at[buf, pl.ds(e * WIN, WIN), :], sem.at[buf])

    @pl.when(i == 0)
    def _():
        for e in range(N_EXPERTS):
            window_copy(0, e, 0).start()

    @pl.when(i + 1 < pl.num_programs(0))
    def _():
        for e in range(N_EXPERTS):
            window_copy(i + 1, e, 1 - par).start()

    slot, gate = slot_ref[...], gate_ref[...]
    lane = lax.broadcasted_iota(jnp.int32, (TT, WIN), 1)
    weights = jnp.concatenate([jnp.where(slot[:, e:e + 1] + off_ref[i * N_EXPERTS + e] == lane,
                                         gate[:, e:e + 1], 0.0).astype(BF16)
                               for e in range(N_EXPERTS)], axis=1)
    for e in range(N_EXPERTS):
        window_copy(i, e, par).wait()
    moe = jnp.dot(weights, ybuf[par], preferred_element_type=F32)
    n_extra = sum(jnp.maximum(off_ref[i * N_EXPERTS + e] + cnt_ref[i * N_EXPERTS + e] - 1, 0) // WIN
                  for e in range(N_EXPERTS))

    def finish(mixture):
        o_ref[...] = x1_ref[...] + ga2_ref[0] * (_rms(mixture) * g_ref[...])

    @pl.when(n_extra == 0)
    def _():
        finish(moe)

    def extra_windows(e, carry):
        k = i * N_EXPERTS + e

        def one(w, c):
            first = (w + 1) * WIN
            cp = pltpu.make_async_copy(
                ye_ref.at[e, pl.ds(pl.multiple_of(start_ref[k] + first, SEG_ALIGN), WIN), :], extra_ref, extra_sem)
            cp.start()
            pick = lax.broadcasted_iota(jnp.int32, (TT, N_EXPERTS), 1) == e
            slot_e = jnp.sum(jnp.where(pick, slot_ref[...], 0), axis=1, keepdims=True)
            gate_e = jnp.sum(jnp.where(pick, gate_ref[...], 0.0), axis=1, keepdims=True)
            wts = jnp.where(slot_e + (off_ref[k] - first) == lane, gate_e, 0.0).astype(BF16)
            cp.wait()
            acc_ref[...] += jnp.dot(wts, extra_ref[...], preferred_element_type=F32)
            return c

        return lax.fori_loop(0, jnp.maximum(off_ref[k] + cnt_ref[k] - 1, 0) // WIN, one, carry)

    @pl.when(n_extra > 0)
    def _():
        acc_ref[...] = moe
        lax.fori_loop(0, N_EXPERTS, extra_windows, 0)
        finish(acc_ref[...])


def _combine(start, off, cnt, slot, gate, x1, ga2, g_post2, ye, tiles_per_seq):
    n_tok, d = x1.shape
    tok = lambda w: pl.BlockSpec((TT, w), lambda i, *_: (i, 0))
    return pl.pallas_call(
        _combine_kernel,
        grid_spec=pltpu.PrefetchScalarGridSpec(
            num_scalar_prefetch=3,
            grid=(n_tok // TT,),
            in_specs=[tok(N_EXPERTS), tok(N_EXPERTS), tok(d),
                      pl.BlockSpec((1, 1, d), lambda i, *_: (i // tiles_per_seq, 0, 0)),
                      pl.BlockSpec((1, d), lambda i, *_: (0, 0)),
                      pl.BlockSpec(memory_space=pl.ANY)],
            out_specs=tok(d),
            scratch_shapes=[pltpu.VMEM((2, N_EXPERTS * WIN, d), BF16), pltpu.VMEM((WIN, d), BF16),
                            pltpu.VMEM((TT, d), F32),
                            pltpu.SemaphoreType.DMA((2,)), pltpu.SemaphoreType.DMA(())]),
        out_shape=jax.ShapeDtypeStruct((n_tok, d), F32),
        compiler_params=pltpu.CompilerParams(dimension_semantics=("arbitrary",), vmem_limit_bytes=VMEM_LIMIT),
        name="route_combine_final",
    )(start, off, cnt, slot, gate, x1, ga2, g_post2.reshape(1, d), ye)


def _moe(groups, g_post2, wg, wu, wd, tm_expert=512):
    d = D_MODEL
    routed = []
    used = 0
    for x1, h2, aff_t, ga2 in groups:
        b, s, _ = x1.shape
        n_tok = b * s
        cap = CAPACITY_FACTOR * n_tok // N_EXPERTS
        assert cap % SEG_ALIGN == 0
        slot_t, slot, gate, cnt = _select(aff_t, _thresholds(aff_t, cap))
        cnt = cnt[:, :, 0]
        base = jnp.cumsum(cnt, axis=0) - cnt + used
        start = base // SEG_ALIGN * SEG_ALIGN
        flat = lambda a: a.reshape(-1).astype(jnp.int32)
        routed.append((flat(start), flat(base - start), flat(cnt), slot_t, slot, gate))
        used += cap
    rows = -(-(used + WIN) // tm_expert) * tm_expert
    xe = jnp.zeros((N_EXPERTS, rows, d), BF16)
    for (x1, h2, _, _), (start, off, cnt, slot_t, _, _) in zip(groups, routed):
        xe = _dispatch(start, off, cnt, slot_t, h2.reshape(-1, d), xe)
    ye = _experts(jnp.full((N_EXPERTS,), used, jnp.int32), xe, wg, wu, wd, tm=tm_expert)
    outs = []
    for (x1, _, _, ga2), (start, off, cnt, _, slot, gate) in zip(groups, routed):
        b, s, _ = x1.shape
        outs.append(_combine(start, off, cnt, slot, gate, x1.reshape(-1, d), ga2, g_post2, ye, s // TT)
                    .reshape(b, s, d))
    return outs


def _hi_lo_rows(w):
    hi = w.astype(BF16)
    return jnp.concatenate([hi, (w - hi.astype(F32)).astype(BF16)], axis=0)


def _rope_tables(s):
    half = D_HEAD // 2
    inv = 1.0 / (ROPE_BASE ** (jnp.arange(half, dtype=F32) / half))
    ang = jnp.arange(s, dtype=F32)[:, None] * inv[None, :]
    cos, sin = jnp.cos(ang), jnp.sin(ang)
    return jnp.concatenate([cos, cos], axis=-1), jnp.concatenate([-sin, sin], axis=-1)


def _token_mixer(x, mod, p):
    b, s, d = x.shape
    sh1, sc1, ga1, sh2, sc2, ga2 = [m[:, None, :] for m in jnp.split(mod, 6, axis=-1)]
    cos2, sin2 = _rope_tables(s)
    u = _inproj(x, sc1, sh1, p["g_pre1"], cos2, sin2, p["w_in"])
    y_ret = _retention(u, p["log_gamma"], p["gn_ret"])
    y_hg = _hgrn2(u, p["lb"], p["gn_hg"], p["hg_consts"])
    x1, h2, aff_t = _merge(x, y_ret, y_hg, u, ga1, sc2, sh2, p["g_post1"], p["g_pre2"],
                           p["w_br_ret"], p["w_br_hg"], p["w_out"], p["w_router_t"])
    return x1, h2, aff_t, ga2


def kernel(x_prompt, x_sample, c_prompt, c_sample, w_ada, b_ada, g_pre1, g_post1, w_in, gn_ret, gn_hg, lb_logits, w_br_ret, w_br_hg, w_out, g_pre2, g_post2, w_router, w_gate, w_up, w_down):
    lower_bounds = jnp.cumsum(jax.nn.softmax(lb_logits.astype(F32), axis=0), axis=0)
    log_gamma = jnp.log(1.0 - jnp.power(2.0, -5.0 - jnp.arange(N_HEADS, dtype=F32)))
    consts = _hg_constants()
    n_gate = 2 * D_MODEL
    xp, xs = x_prompt, x_sample
    bp, bs = xp.shape[0], xs.shape[0]
    for l in range(w_ada.shape[0]):
        c_all = jnp.concatenate([c_prompt, c_sample, jnp.zeros((-(bp + bs) % 8, D_MODEL), F32)], axis=0)
        mod = _ada(c_all, w_ada[l], b_ada[l])
        w_in_l = w_in[l]
        p = dict(
            g_pre1=g_pre1[l], g_post1=g_post1[l], g_pre2=g_pre2[l], g_post2=g_post2[l],
            w_in=jnp.concatenate([w_in_l[:, -n_gate:], w_in_l[:, :-n_gate]], axis=1).astype(BF16),
            log_gamma=jnp.broadcast_to(log_gamma[:, None, None], (N_HEADS, 1, LANES)),
            gn_ret=gn_ret[l], gn_hg=gn_hg[l], lb=lower_bounds[l], hg_consts=consts,
            w_br_ret=w_br_ret[l].astype(BF16), w_br_hg=w_br_hg[l].astype(BF16), w_out=w_out[l].astype(BF16),
            w_router_t=_hi_lo_rows(w_router[l].T),
            w_gate=w_gate[l], w_up=w_up[l], w_down=w_down[l],
        )
        mixed = [_token_mixer(xp, mod[:bp], p), _token_mixer(xs, mod[bp:bp + bs], p)]
        xp, xs = _moe(mixed, p["g_post2"], p["w_gate"], p["w_up"], p["w_down"])
    return (xp, xs)
```

```python
import functools

import numpy as np
import jax
import jax.numpy as jnp
from jax import lax
from jax.experimental import pallas as pl
from jax.experimental.pallas import tpu as pltpu

F32 = jnp.float32
BF16 = jnp.bfloat16

D_MODEL = 1024
N_HEADS = 4
D_HEAD = 128
N_EXPERTS = 16
CAPACITY_FACTOR = 2
D_EXPERT = 2048
ROPE_BASE = 10000.0
NORM_EPS = 1e-6
CHUNK = 128
LANES = 128
VMEM_LIMIT = 56 * 1024 * 1024

D_IN = 2 * D_MODEL + 9 * N_HEADS * D_HEAD
COL_RETQ, COL_RETK, COL_RETV, COL_RETG = 16, 20, 24, 28
COL_HGQ, COL_HGI, COL_HGFF, COL_HGFB, COL_HGG = 32, 36, 40, 44, 48
TN_IN = 512
ROPE_CHUNKS = (COL_RETQ * LANES // TN_IN, COL_RETK * LANES // TN_IN)

LOG2_E = 1.4426950408889634
MIN_NORMAL = 2.0 ** -126
NT_DIMS = (((1,), (1,)), ((), ()))
TN_DIMS = (((0,), (0,)), ((), ()))


def _sigmoid(x):
    return 1.0 / (1.0 + jnp.exp(-x))


def _rms(x, eps=NORM_EPS):
    return x * lax.rsqrt(jnp.mean(x * x, axis=-1, keepdims=True) + eps)


def _ada_kernel(c_ref, w_ref, b_ref, o_ref):
    c = c_ref[...]
    o_ref[...] = jnp.dot(c * _sigmoid(c), w_ref[...], preferred_element_type=F32,
                         precision=lax.Precision.HIGHEST) + b_ref[...]


def _ada(c_all, w_ada, b_ada):
    rows = c_all.shape[0]
    n_out = w_ada.shape[1]
    tn = 1024
    return pl.pallas_call(
        _ada_kernel,
        grid=(n_out // tn,),
        in_specs=[pl.BlockSpec((rows, D_MODEL), lambda n: (0, 0)),
                  pl.BlockSpec((D_MODEL, tn), lambda n: (0, n)),
                  pl.BlockSpec((1, tn), lambda n: (0, n))],
        out_specs=pl.BlockSpec((rows, tn), lambda n: (0, n)),
        out_shape=jax.ShapeDtypeStruct((rows, n_out), F32),
        name="ada_mod",
    )(c_all, w_ada, b_ada.reshape(1, n_out))


def _inproj_kernel(x_ref, sc_ref, sh_ref, g_ref, cos_ref, sin_ref, w_ref, u_ref):
    h = (_rms(x_ref[0]) * (g_ref[...] * (1.0 + sc_ref[0])) + sh_ref[0]).astype(BF16)
    for n in range(D_IN // TN_IN):
        acc = jnp.dot(h, w_ref[:, n * TN_IN:(n + 1) * TN_IN], preferred_element_type=F32)
        if n in ROPE_CHUNKS:
            scale = 1.0 if n == ROPE_CHUNKS[0] else D_HEAD ** -0.5
            for hd in range(TN_IN // D_HEAD):
                blk = acc[:, hd * D_HEAD:(hd + 1) * D_HEAD]
                rot = blk * cos_ref[...] + pltpu.roll(blk, D_HEAD // 2, axis=1) * sin_ref[...]
                lo = n * TN_IN + hd * D_HEAD
                u_ref[0, :, lo:lo + D_HEAD] = (rot * scale).astype(BF16)
        else:
            u_ref[0, :, n * TN_IN:(n + 1) * TN_IN] = acc.astype(BF16)


def _inproj(x, sc, sh, g, cos2, sin2, w_bf16, tm=512):
    b, s, d = x.shape
    return pl.pallas_call(
        _inproj_kernel,
        grid=(b, s // tm),
        in_specs=[pl.BlockSpec((1, tm, d), lambda i, j: (i, j, 0)),
                  pl.BlockSpec((1, 1, d), lambda i, j: (i, 0, 0)),
                  pl.BlockSpec((1, 1, d), lambda i, j: (i, 0, 0)),
                  pl.BlockSpec((1, d), lambda i, j: (0, 0)),
                  pl.BlockSpec((tm, D_HEAD), lambda i, j: (j, 0)),
                  pl.BlockSpec((tm, D_HEAD), lambda i, j: (j, 0)),
                  pl.BlockSpec((d, D_IN), lambda i, j: (0, 0))],
        out_specs=pl.BlockSpec((1, tm, D_IN), lambda i, j: (i, j, 0)),
        out_shape=jax.ShapeDtypeStruct((b, s, D_IN), BF16),
        compiler_params=pltpu.CompilerParams(dimension_semantics=("parallel", "parallel"),
                                             vmem_limit_bytes=VMEM_LIMIT),
        name="inproj",
    )(x, sc, sh, g.reshape(1, d), cos2, sin2, w_bf16)


def _ret_kernel(q_ref, k_ref, v_ref, g_ref, lg_ref, gn_ref, o_ref, yf_ref, yb_ref, *, n_chunks):
    c = CHUNK
    lg = lg_ref[0]
    pos = lax.broadcasted_iota(jnp.int32, (c, LANES), 0).astype(F32)
    dist = jnp.abs(lax.broadcasted_iota(jnp.int32, (c, c), 0)
                   - lax.broadcasted_iota(jnp.int32, (c, c), 1)).astype(F32)
    decay = jnp.exp(lg * dist)
    q_fwd = jnp.exp(lg * (pos + 1.0)).astype(BF16)
    k_fwd = jnp.exp(lg * (c - 1.0 - pos)).astype(BF16)
    q_bwd = jnp.exp(lg * (c - pos)).astype(BF16)
    k_bwd = jnp.exp(lg * pos).astype(BF16)
    g_chunk = jnp.exp(lg * c)

    def scaled(t, f):
        return t * f

    def sweep(n, carry):
        r_f, r_b = carry
        sl = pl.ds(pl.multiple_of(n * c, c), c)
        q, k, v = q_ref[0, sl, :], k_ref[0, sl, :], v_ref[0, sl, :]
        s = lax.dot_general(q, k, NT_DIMS, preferred_element_type=F32) * decay
        lhs = jnp.concatenate([s.astype(BF16), scaled(q, q_fwd)], axis=1)
        rhs = jnp.concatenate([v, r_f.astype(BF16)], axis=0)
        yf_ref[sl, :] = jnp.dot(lhs, rhs, preferred_element_type=F32)
        r_f = g_chunk * r_f + lax.dot_general(scaled(k, k_fwd), v, TN_DIMS, preferred_element_type=F32)
        sl = pl.ds(pl.multiple_of((n_chunks - 1 - n) * c, c), c)
        q, k, v = q_ref[0, sl, :], k_ref[0, sl, :], v_ref[0, sl, :]
        yb_ref[sl, :] = jnp.dot(scaled(q, q_bwd), r_b.astype(BF16), preferred_element_type=F32)
        r_b = g_chunk * r_b + lax.dot_general(scaled(k, k_bwd), v, TN_DIMS, preferred_element_type=F32)
        return r_f, r_b

    zero = jnp.zeros((D_HEAD, D_HEAD), F32)
    lax.fori_loop(0, n_chunks, sweep, (zero, zero), unroll=8)

    def finish(n, carry):
        sl = pl.ds(pl.multiple_of(n * c, c), c)
        y = yf_ref[sl, :] + yb_ref[sl, :]
        yc = y - jnp.mean(y, axis=-1, keepdims=True)
        g = g_ref[0, sl, :].astype(F32)
        o_ref[0, sl, :] = (_rms(yc) * gn_ref[0] * (g * _sigmoid(g))).astype(BF16)
        return carry

    lax.fori_loop(0, n_chunks, finish, 0, unroll=4)


def _retention(u, log_gamma, gn):
    b, s, _ = u.shape

    def col(c0):
        return pl.BlockSpec((1, s, D_HEAD), lambda i, h: (i, 0, c0 + h))

    per_head = pl.BlockSpec((1, 1, D_HEAD), lambda i, h: (h, 0, 0))
    return pl.pallas_call(
        functools.partial(_ret_kernel, n_chunks=s // CHUNK),
        grid=(b, N_HEADS),
        in_specs=[col(COL_RETQ), col(COL_RETK), col(COL_RETV), col(COL_RETG), per_head, per_head],
        out_specs=pl.BlockSpec((1, s, D_HEAD), lambda i, h: (i, 0, h)),
        out_shape=jax.ShapeDtypeStruct((b, s, N_HEADS * D_HEAD), BF16),
        scratch_shapes=[pltpu.VMEM((s, D_HEAD), F32), pltpu.VMEM((s, D_HEAD), F32)],
        compiler_params=pltpu.CompilerParams(dimension_semantics=("parallel", "parallel"),
                                             vmem_limit_bytes=VMEM_LIMIT),
        name="retention",
    )(u, u, u, u, log_gamma, gn.reshape(N_HEADS, 1, D_HEAD))


N_LEVELS = 8
ROW_PIECE = 16
N_SPLIT_LEVEL = 5


def _hg_constants():
    c = CHUNK
    p = np.arange(c)
    tril = (p[None, :] <= p[:, None]).astype(np.float64)
    masks = [np.eye(c)]
    for l in range(1, N_LEVELS):
        s, half = 2 ** l, 2 ** (l - 1)
        blk = p // s
        upper = (p % s) >= half
        masks.append(((blk[:, None] == blk[None, :]) & upper[:, None] & (~upper)[None, :]).astype(np.float64))
    mask_f = np.stack(masks)
    mask_b = np.stack([m.T for m in masks])
    tile2 = lambda m: np.concatenate([m, m], axis=1)
    return (jnp.asarray(tile2(tril), BF16), jnp.asarray(tile2(tril[::-1, ::-1]), BF16),
            jnp.asarray(mask_f, BF16), jnp.asarray(mask_b, BF16))


def _neg_block_offsets(cum, level, forward):
    s = 2 ** level
    half = s // 2
    blocks = cum.reshape(CHUNK // s, s, LANES)
    lower, upper = blocks[:, :half, :], blocks[:, half:, :]
    if forward:
        ref = blocks[:, half - 1:half, :]
        out = jnp.concatenate([ref - lower, upper - ref], axis=1)
    else:
        ref = blocks[:, half:half + 1, :]
        out = jnp.concatenate([lower - ref, ref - upper], axis=1)
    return out.reshape(CHUNK, LANES)


SUBLANES = 8


def _neg_small_offsets(cum, lf, level, forward):
    s = 2 ** level
    half = s // 2
    row = lax.broadcasted_iota(jnp.int32, (CHUNK // SUBLANES, SUBLANES, LANES), 1)
    in_lower = (row & half) == 0
    if level == 1:
        keep = jnp.logical_not(in_lower) if forward else in_lower
        return jnp.where(keep.reshape(CHUNK, LANES), lf, 0.0)
    rows = cum.reshape(CHUNK // SUBLANES, SUBLANES, LANES)
    first = half - 1 if forward else half
    ref = rows[:, first:first + 1, :]
    for blk in range(1, SUBLANES // s):
        r = blk * s + first
        ref = jnp.where(row < blk * s, ref, rows[:, r:r + 1, :])
    diff = rows - ref
    return jnp.where(in_lower == forward, -diff, diff).reshape(CHUNK, LANES)


def _hg_gates(z, lb, mat_ref):
    sg = _sigmoid(z)
    lf = jnp.log(jnp.maximum(lb + (1.0 - lb) * sg, MIN_NORMAL)) * LOG2_E
    kb = ((1.0 - lb) * (1.0 - sg)).astype(BF16)
    hi = lf.astype(BF16)
    mid = (lf - hi.astype(F32)).astype(BF16)
    return kb, lf, jnp.dot(mat_ref[...], jnp.concatenate([hi, mid], axis=0), preferred_element_type=F32)


def _block_diag(a, b):
    zero = jnp.zeros_like(a)
    return jnp.concatenate([jnp.concatenate([a, zero], axis=1), jnp.concatenate([zero, b], axis=1)], axis=0)


def _hg_pair(fwd, bwd, lb, mats, masks):
    c = CHUNK
    dirs = (True, False)
    q = [fwd[0], bwd[0]]
    v = [fwd[1], bwd[1]]
    state = [fwd[3], bwd[3]]
    kb, lf, cum = zip(*[_hg_gates(d[2], lb, m) for d, m in zip((fwd, bwd), mats)])
    piece = lambda t, i: t[ROW_PIECE * i:ROW_PIECE * (i + 1)]
    n_pieces = c // ROW_PIECE
    halves = lambda p: (p[:, :c], p[:, c:])

    def paired_nt(lhs, rhs):
        return halves(lax.dot_general(jnp.concatenate(lhs, axis=1), _block_diag(*rhs), NT_DIMS,
                                      preferred_element_type=F32))

    def scale(l):
        neg = [_neg_small_offsets(cum[d], lf[d], l, dirs[d]) if 2 ** l <= SUBLANES
               else _neg_block_offsets(cum[d], l, dirs[d]) for d in range(2)]
        return [jnp.exp2(n).astype(BF16) for n in neg]

    a = [p.astype(BF16) * m[0] for p, m in zip(paired_nt(q, kb), masks)]
    for l in range(1, N_SPLIT_LEVEL):
        e = scale(l)
        p = paired_nt([q[d] * e[d] for d in range(2)], [kb[d] * e[d] for d in range(2)])
        a = [a[d] + p[d].astype(BF16) * masks[d][l] for d in range(2)]
    a = [[piece(a[d], i) for i in range(n_pieces)] for d in range(2)]
    for l in range(N_SPLIT_LEVEL, N_LEVELS):
        e = scale(l)
        per_half = 2 ** (l - 1) // ROW_PIECE
        is_query = [[((i // per_half) % 2 == 1) == dirs[d] for i in range(n_pieces)] for d in range(2)]
        queries = [[i for i in range(n_pieces) if is_query[d][i]] for d in range(2)]
        q_rows = [jnp.concatenate([piece(q[d], i) * piece(e[d], i) for i in queries[d]], axis=0) for d in range(2)]
        k_rows = [jnp.concatenate([piece(kb[d], i) if is_query[d][i] else piece(kb[d], i) * piece(e[d], i)
                                   for i in range(n_pieces)], axis=0) for d in range(2)]
        p = paired_nt(q_rows, k_rows)
        for d in range(2):
            for n, i in enumerate(queries[d]):
                a[d][i] = a[d][i] + (piece(p[d], n).astype(BF16)
                                     * masks[d][l, ROW_PIECE * i:ROW_PIECE * (i + 1), :])
    a = [jnp.concatenate(a[d], axis=0) for d in range(2)]
    y = halves(jnp.dot(jnp.concatenate(a, axis=1), _block_diag(*v), preferred_element_type=F32))
    carried = paired_nt([q[d] * jnp.exp2(cum[d]).astype(BF16) for d in range(2)],
                        [state[d].astype(BF16) for d in range(2)])
    out = []
    for d in range(2):
        total = cum[d][c - 1:c] if dirs[d] else cum[d][0:1]
        kd = kb[d] * jnp.exp2(total - cum[d]).astype(BF16)
        new_state = state[d] * jnp.exp2(total) + lax.dot_general(v[d], kd, TN_DIMS, preferred_element_type=F32)
        out.append((y[d] + carried[d], new_state))
    return out


def _hg_kernel(q_ref, i_ref, zf_ref, zb_ref, g_ref, lb_ref, gn_ref, matf_ref, matb_ref, maskf_ref, maskb_ref,
               o_ref, yf_ref, yb_ref, *, n_chunks):
    c = CHUNK
    lb = lb_ref[0]

    def sweep(n, carry):
        s_f, s_b = carry
        sf = pl.ds(pl.multiple_of(n * c, c), c)
        sb = pl.ds(pl.multiple_of((n_chunks - 1 - n) * c, c), c)
        (yf_ref[sf, :], s_f), (yb_ref[sb, :], s_b) = _hg_pair(
            (q_ref[0, sf, :], i_ref[0, sf, :], zf_ref[0, sf, :].astype(F32), s_f),
            (q_ref[0, sb, :], i_ref[0, sb, :], zb_ref[0, sb, :].astype(F32), s_b),
            lb, (matf_ref, matb_ref), (maskf_ref, maskb_ref))
        return s_f, s_b

    zero = jnp.zeros((D_HEAD, D_HEAD), F32)
    lax.fori_loop(0, n_chunks, sweep, (zero, zero), unroll=8)

    def finish(n, carry):
        sl = pl.ds(pl.multiple_of(n * c, c), c)
        g = g_ref[0, sl, :].astype(F32)
        o_ref[0, sl, :] = (_rms(yf_ref[sl, :] + yb_ref[sl, :]) * gn_ref[0] * (g * _sigmoid(g))).astype(BF16)
        return carry

    lax.fori_loop(0, n_chunks, finish, 0, unroll=4)


def _hgrn2(u, lb, gn, consts):
    b, s, _ = u.shape
    mat_f, mat_b, mask_f, mask_b = consts

    def col(c0):
        return pl.BlockSpec((1, s, D_HEAD), lambda i, h: (i, 0, c0 + h))

    def whole(a):
        return pl.BlockSpec(a.shape, lambda i, h: (0,) * a.ndim)

    per_head = pl.BlockSpec((1, 1, D_HEAD), lambda i, h: (h, 0, 0))
    return pl.pallas_call(
        functools.partial(_hg_kernel, n_chunks=s // CHUNK),
        grid=(b, N_HEADS),
        in_specs=[col(COL_HGQ), col(COL_HGI), col(COL_HGFF), col(COL_HGFB), col(COL_HGG), per_head, per_head,
                  whole(mat_f), whole(mat_b), whole(mask_f), whole(mask_b)],
        out_specs=pl.BlockSpec((1, s, D_HEAD), lambda i, h: (i, 0, h)),
        out_shape=jax.ShapeDtypeStruct((b, s, N_HEADS * D_HEAD), BF16),
        scratch_shapes=[pltpu.VMEM((s, D_HEAD), F32), pltpu.VMEM((s, D_HEAD), F32)],
        compiler_params=pltpu.CompilerParams(dimension_semantics=("parallel", "parallel"),
                                             vmem_limit_bytes=VMEM_LIMIT),
        name="hgrn2",
    )(u, u, u, u, u, lb.reshape(N_HEADS, 1, D_HEAD), gn.reshape(N_HEADS, 1, D_HEAD),
      mat_f, mat_b, mask_f, mask_b)


def _merge_kernel(x_ref, yr_ref, yh_ref, za_ref, zb_ref, ga1_ref, sc2_ref, sh2_ref, gpost1_ref, gpre2_ref,
                  wr_ref, wh_ref, wo_ref, wrt_ref, x1_ref, h2_ref, aff_ref):
    br = jnp.dot(yr_ref[0], wr_ref[...], preferred_element_type=F32)
    bh = jnp.dot(yh_ref[0], wh_ref[...], preferred_element_type=F32)
    merged = _sigmoid(za_ref[0].astype(F32)) * br + _sigmoid(zb_ref[0].astype(F32)) * bh
    o = jnp.dot(merged.astype(BF16), wo_ref[...], preferred_element_type=F32)
    x1 = x_ref[0] + _rms(o) * (ga1_ref[0] * gpost1_ref[...])
    x1_ref[0] = x1
    h2 = _rms(x1) * (gpre2_ref[...] * (1.0 + sc2_ref[0])) + sh2_ref[0]
    h2_hi = h2.astype(BF16)
    h2_ref[0] = h2_hi
    h2_lo = (h2 - h2_hi.astype(F32)).astype(BF16)
    w_parts = wrt_ref[...]
    by_hi = lax.dot_general(w_parts, h2_hi, NT_DIMS, preferred_element_type=F32)
    logits = (by_hi[:N_EXPERTS] + by_hi[N_EXPERTS:]
              + lax.dot_general(w_parts[:N_EXPERTS], h2_lo, NT_DIMS, preferred_element_type=F32))
    ex = jnp.exp(logits - jnp.max(logits, axis=0, keepdims=True))
    aff_ref[...] = ex / jnp.sum(ex, axis=0, keepdims=True)


def _merge(x, y_ret, y_hg, u, ga1, sc2, sh2, g_post1, g_pre2, w_br_ret, w_br_hg, w_out, w_router_t, tm=512):
    b, s, d = x.shape
    nj = s // tm
    tok = lambda w: pl.BlockSpec((1, tm, w), lambda i, j: (i, j, 0))
    per_b = pl.BlockSpec((1, 1, d), lambda i, j: (i, 0, 0))
    vec = pl.BlockSpec((1, d), lambda i, j: (0, 0))
    whole = lambda a: pl.BlockSpec(a.shape, lambda i, j: (0,) * a.ndim)
    return pl.pallas_call(
        _merge_kernel,
        grid=(b, nj),
        in_specs=[tok(d), tok(N_HEADS * D_HEAD), tok(N_HEADS * D_HEAD),
                  pl.BlockSpec((1, tm, d), lambda i, j: (i, j, 0)),
                  pl.BlockSpec((1, tm, d), lambda i, j: (i, j, 1)),
                  per_b, per_b, per_b, vec, vec,
                  whole(w_br_ret), whole(w_br_hg), whole(w_out), whole(w_router_t)],
        out_specs=[tok(d), tok(d), pl.BlockSpec((N_EXPERTS, tm), lambda i, j: (0, i * nj + j))],
        out_shape=[jax.ShapeDtypeStruct((b, s, d), F32), jax.ShapeDtypeStruct((b, s, d), BF16),
                   jax.ShapeDtypeStruct((N_EXPERTS, b * s), F32)],
        compiler_params=pltpu.CompilerParams(dimension_semantics=("parallel", "parallel"),
                                             vmem_limit_bytes=VMEM_LIMIT),
        name="merge_outproj_router",
    )(x, y_ret, y_hg, u, u, ga1, sc2, sh2, g_post1.reshape(1, d), g_pre2.reshape(1, d),
      w_br_ret, w_br_hg, w_out, w_router_t)


TF_EXPERT = 512


N_FF_CHUNKS = D_EXPERT // TF_EXPERT


def _expert_kernel(total_ref, x_ref, wg_ref, wu_ref, wd_ref, o_ref, wg_bf, wu_bf, wd_bf, *, tm, n_tiles):
    s, j = pl.program_id(0), pl.program_id(1)
    fill, use = s % 2, 1 - s % 2
    for c in range(3 * N_FF_CHUNKS):
        @pl.when(jnp.logical_and(j == c, s < pl.num_programs(0) - 1))
        def _():
            which, f = divmod(c, N_FF_CHUNKS)
            src, dst = ((wg_ref, wg_bf), (wu_ref, wu_bf), (wd_ref, wd_bf))[which]
            dst[fill, f] = src[0].astype(BF16)

    live_rows = jnp.where(jnp.logical_and(s > 0, j < n_tiles), total_ref[jnp.maximum(s - 1, 0)] - j * tm, 0)
    half = tm // 2

    def swiglu(x):
        acc = jnp.zeros((x.shape[0], D_MODEL), F32)
        for f in range(N_FF_CHUNKS):
            a = jnp.dot(x, wg_bf[use, f], preferred_element_type=F32)
            up = jnp.dot(x, wu_bf[use, f], preferred_element_type=F32)
            acc += jnp.dot((a * _sigmoid(a) * up).astype(BF16), wd_bf[use, f], preferred_element_type=F32)
        return acc.astype(BF16)

    @pl.when(live_rows > half)
    def _():
        o_ref[0] = swiglu(x_ref[0])

    @pl.when(jnp.logical_and(live_rows > 0, live_rows <= half))
    def _():
        o_ref[0, :half, :] = swiglu(x_ref[0, :half, :])
        o_ref[0, half:, :] = jnp.zeros((half, D_MODEL), BF16)

    @pl.when(jnp.logical_and(live_rows <= 0, j < n_tiles))
    def _():
        o_ref[0] = jnp.zeros((tm, D_MODEL), BF16)


def _experts(total, xe, wg, wu, wd, tm=512):
    e, rows, d = xe.shape
    n_tiles = rows // tm
    steps = max(n_tiles, 3 * N_FF_CHUNKS)
    nxt = lambda s: jnp.minimum(s, e - 1)
    cur = lambda s: jnp.maximum(s - 1, 0)
    tile = lambda j: jnp.minimum(j, n_tiles - 1)
    chunk = lambda j, which: jnp.clip(j - which * N_FF_CHUNKS, 0, N_FF_CHUNKS - 1)
    return pl.pallas_call(
        functools.partial(_expert_kernel, tm=tm, n_tiles=n_tiles),
        grid_spec=pltpu.PrefetchScalarGridSpec(
            num_scalar_prefetch=1,
            grid=(e + 1, steps),
            in_specs=[pl.BlockSpec((1, tm, d), lambda s, j, t: (cur(s), tile(j), 0)),
                      pl.BlockSpec((1, d, TF_EXPERT), lambda s, j, t: (nxt(s), 0, chunk(j, 0))),
                      pl.BlockSpec((1, d, TF_EXPERT), lambda s, j, t: (nxt(s), 0, chunk(j, 1))),
                      pl.BlockSpec((1, TF_EXPERT, d), lambda s, j, t: (nxt(s), chunk(j, 2), 0))],
            out_specs=pl.BlockSpec((1, tm, d), lambda s, j, t: (jnp.where(s == 0, e, s - 1), tile(j), 0)),
            scratch_shapes=[pltpu.VMEM((2, N_FF_CHUNKS, d, TF_EXPERT), BF16),
                            pltpu.VMEM((2, N_FF_CHUNKS, d, TF_EXPERT), BF16),
                            pltpu.VMEM((2, N_FF_CHUNKS, TF_EXPERT, d), BF16)]),
        out_shape=jax.ShapeDtypeStruct((e + 1, rows, d), BF16),
        compiler_params=pltpu.CompilerParams(dimension_semantics=("arbitrary", "arbitrary"),
                                             vmem_limit_bytes=VMEM_LIMIT),
        name="expert_ffn",
    )(total, xe, wg, wu, wd)


TT = 512
WIN = 128
SEG_ALIGN = 16
NOT_CHOSEN = -float(1 << 20)


MANTISSA_STEPS = 53


def _threshold_kernel(a_ref, o_ref, *, cap):
    def count_ge(x):
        return jnp.sum(jnp.where(a_ref[...] >= x, 1.0, 0.0), axis=1, keepdims=True)

    lo = jnp.where(count_ge(MIN_NORMAL) >= cap, MIN_NORMAL, 0.0) * jnp.ones((N_EXPERTS, 1), F32)
    for j in (64, 32, 16, 8, 4, 2, 1):
        cand = lo * (2.0 ** j)
        lo = jnp.where(count_ge(cand) >= cap, cand, lo)
    hi = jnp.where(lo > 0.0, 2.0 * lo, MIN_NORMAL)

    def halve(_, bounds):
        lo, hi = bounds
        mid = 0.5 * (lo + hi)
        ok = count_ge(mid) >= cap
        return jnp.where(ok, mid, lo), jnp.where(ok, hi, mid)

    lo, hi = lax.fori_loop(0, MANTISSA_STEPS, halve, (lo, hi))
    a = a_ref[...]
    thr = jnp.min(jnp.where(a >= lo, a, 2.0), axis=1, keepdims=True)
    n_gt = jnp.sum(jnp.where(a > thr, 1.0, 0.0), axis=1, keepdims=True)
    o_ref[0] = jnp.broadcast_to(thr, (N_EXPERTS, LANES))
    o_ref[1] = jnp.broadcast_to(cap - n_gt, (N_EXPERTS, LANES))


def _thresholds(aff_t, cap):
    return pl.pallas_call(
        functools.partial(_threshold_kernel, cap=cap),
        out_shape=jax.ShapeDtypeStruct((2, N_EXPERTS, LANES), F32),
        compiler_params=pltpu.CompilerParams(vmem_limit_bytes=VMEM_LIMIT),
        name="route_threshold",
    )(aff_t)


def _select_kernel(a_ref, thr_ref, upper_ref, slot_t_ref, slot_ref, gate_ref, cnt_ref, seen_ref):
    @pl.when(pl.program_id(0) == 0)
    def _():
        seen_ref[...] = jnp.zeros_like(seen_ref)

    a = a_ref[...]
    thr = thr_ref[0][:, 0:1]
    need = thr_ref[1][:, 0:1]
    gt, eq = a > thr, a == thr
    marks = jnp.concatenate([jnp.where(gt, 1.0, 0.0), jnp.where(eq, 1.0, 0.0)], axis=0).astype(BF16)
    before = jnp.dot(marks, upper_ref[...], preferred_element_type=F32)
    gt_before, eq_before = before[:N_EXPERTS], before[N_EXPERTS:]
    seen = seen_ref[:, 0:1]
    eq_rank = seen + eq_before
    sel = jnp.logical_or(gt, jnp.logical_and(eq, eq_rank < need))
    rank = gt_before + jnp.minimum(eq_rank, need) - jnp.minimum(seen, need)
    slot_t = jnp.where(sel, rank, NOT_CHOSEN)
    slot_t_ref[...] = slot_t.astype(jnp.int32)
    cnt_ref[0] = jnp.broadcast_to(jnp.sum(jnp.where(sel, 1, 0), axis=1, keepdims=True), (N_EXPERTS, LANES))
    seen_ref[...] = seen_ref[...] + jnp.sum(jnp.where(eq, 1.0, 0.0), axis=1, keepdims=True)
    pad = jnp.zeros((LANES - N_EXPERTS, TT), F32)
    slot_ref[...] = jnp.concatenate([slot_t, pad], axis=0).T[:, :N_EXPERTS].astype(jnp.int32)
    gate_ref[...] = jnp.concatenate([a, pad], axis=0).T[:, :N_EXPERTS]


def _select(aff_t, thr):
    n_tok = aff_t.shape[1]
    n_tiles = n_tok // TT
    t = np.arange(TT)
    upper = jnp.asarray(t[:, None] < t[None, :], BF16)
    return pl.pallas_call(
        _select_kernel,
        grid=(n_tiles,),
        in_specs=[pl.BlockSpec((N_EXPERTS, TT), lambda i: (0, i)),
                  pl.BlockSpec((2, N_EXPERTS, LANES), lambda i: (0, 0, 0)),
                  pl.BlockSpec((TT, TT), lambda i: (0, 0))],
        out_specs=[pl.BlockSpec((N_EXPERTS, TT), lambda i: (0, i)),
                   pl.BlockSpec((TT, N_EXPERTS), lambda i: (i, 0)),
                   pl.BlockSpec((TT, N_EXPERTS), lambda i: (i, 0)),
                   pl.BlockSpec((1, N_EXPERTS, LANES), lambda i: (i, 0, 0))],
        out_shape=[jax.ShapeDtypeStruct((N_EXPERTS, n_tok), jnp.int32),
                   jax.ShapeDtypeStruct((n_tok, N_EXPERTS), jnp.int32),
                   jax.ShapeDtypeStruct((n_tok, N_EXPERTS), F32),
                   jax.ShapeDtypeStruct((n_tiles, N_EXPERTS, LANES), jnp.int32)],
        scratch_shapes=[pltpu.VMEM((N_EXPERTS, LANES), F32)],
        compiler_params=pltpu.CompilerParams(dimension_semantics=("arbitrary",)),
        name="route_select",
    )(aff_t, thr, upper)


def _dispatch_kernel(start_ref, off_ref, cnt_ref, slot_ref, h_ref, xe_in_ref, xe_ref,
                     stage_ref, extra_ref, carry_ref, sem, extra_sem):
    del xe_in_ref
    i = pl.program_id(0)
    par = i % 2
    tiles_per_win = WIN // SEG_ALIGN

    @pl.when(i == 0)
    def _():
        carry_ref[...] = jnp.zeros_like(carry_ref)

    row = lax.broadcasted_iota(jnp.int32, (WIN, TT), 0)
    slot = slot_ref[...]
    onehot = jnp.concatenate([jnp.where(slot[e:e + 1, :] + off_ref[i * N_EXPERTS + e] == row, 1.0, 0.0).astype(BF16)
                              for e in range(N_EXPERTS)], axis=0)
    stage_ref[par] = jnp.dot(onehot, h_ref[...], preferred_element_type=F32).astype(BF16)

    def window_copy(e):
        start = pl.multiple_of(start_ref[i * N_EXPERTS + e], SEG_ALIGN)
        return pltpu.make_async_copy(stage_ref.at[par, pl.ds(e * WIN, WIN), :],
                                     xe_ref.at[e, pl.ds(start, WIN), :], sem.at[e])

    for e in range(N_EXPERTS):
        end = off_ref[i * N_EXPERTS + e] + cnt_ref[i * N_EXPERTS + e]
        head = pl.ds(e * WIN, SEG_ALIGN)
        stage_ref[par, head, :] = stage_ref[par, head, :] + carry_ref[e]
        tile = jnp.minimum(end // SEG_ALIGN, tiles_per_win - 1)
        last = stage_ref[par, pl.ds(pl.multiple_of(e * WIN + tile * SEG_ALIGN, SEG_ALIGN), SEG_ALIGN), :]
        carry_ref[e] = jnp.where(end < WIN, last, jnp.zeros_like(last))

        @pl.when(i > 0)
        def _():
            window_copy(e).wait()

        window_copy(e).start()

    def extra_windows(e, carry):
        k = i * N_EXPERTS + e
        end = off_ref[k] + cnt_ref[k]

        def one(w, c):
            first = (w + 1) * WIN
            srow = slot_ref[pl.ds(e, 1), :] + (off_ref[k] - first)
            oh = jnp.where(srow == row, 1.0, 0.0).astype(BF16)
            extra_ref[...] = jnp.dot(oh, h_ref[...], preferred_element_type=F32).astype(BF16)
            cp = pltpu.make_async_copy(
                extra_ref, xe_ref.at[e, pl.ds(pl.multiple_of(start_ref[k] + first, SEG_ALIGN), WIN), :], extra_sem)
            cp.start()
            tile = end // SEG_ALIGN - (w + 1) * tiles_per_win
            inside = jnp.logical_and(tile >= 0, tile < tiles_per_win)
            last = extra_ref[pl.ds(pl.multiple_of(jnp.clip(tile, 0, tiles_per_win - 1) * SEG_ALIGN, SEG_ALIGN),
                                   SEG_ALIGN), :]
            carry_ref[e] = jnp.where(inside, last, carry_ref[e])
            cp.wait()
            return c

        return lax.fori_loop(0, jnp.maximum(end - 1, 0) // WIN, one, carry)

    lax.fori_loop(0, N_EXPERTS, extra_windows, 0)

    @pl.when(i == pl.num_programs(0) - 1)
    def _():
        for e in range(N_EXPERTS):
            window_copy(e).wait()


def _dispatch(start, off, cnt, slot_t, h2, xe0):
    n_tok, d = h2.shape
    rows = xe0.shape[1]
    return pl.pallas_call(
        _dispatch_kernel,
        grid_spec=pltpu.PrefetchScalarGridSpec(
            num_scalar_prefetch=3,
            grid=(n_tok // TT,),
            in_specs=[pl.BlockSpec((N_EXPERTS, TT), lambda i, *_: (0, i)),
                      pl.BlockSpec((TT, d), lambda i, *_: (i, 0)),
                      pl.BlockSpec(memory_space=pl.ANY)],
            out_specs=pl.BlockSpec(memory_space=pl.ANY),
            scratch_shapes=[pltpu.VMEM((2, N_EXPERTS * WIN, d), BF16), pltpu.VMEM((WIN, d), BF16),
                            pltpu.VMEM((N_EXPERTS, SEG_ALIGN, d), BF16),
                            pltpu.SemaphoreType.DMA((N_EXPERTS,)), pltpu.SemaphoreType.DMA(())]),
        out_shape=jax.ShapeDtypeStruct((N_EXPERTS, rows, d), BF16),
        input_output_aliases={5: 0},
        compiler_params=pltpu.CompilerParams(dimension_semantics=("arbitrary",), vmem_limit_bytes=VMEM_LIMIT),
        name="route_dispatch",
    )(start, off, cnt, slot_t, h2, xe0)


def _combine_kernel(start_ref, off_ref, cnt_ref, slot_ref, gate_ref, x1_ref, ga2_ref, g_ref, ye_ref, o_ref,
                    ybuf, extra_ref, acc_ref, sem, extra_sem):
    i = pl.program_id(0)
    par = i % 2

    def window_copy(step, e, buf):
        start = pl.multiple_of(start_ref[step * N_EXPERTS + e], SEG_ALIGN)
        return pltpu.make_async_copy(ye_ref.at[e, pl.ds(start, WIN), :],
                                     ybuf.at[buf, pl.ds(e * WIN, WIN), :], sem.at[buf])

    @pl.when(i == 0)
    def _():
        for e in range(N_EXPERTS):
            window_copy(0, e, 0).start()

    @pl.when(i + 1 < pl.num_programs(0))
    def _():
        for e in range(N_EXPERTS):
            window_copy(i + 1, e, 1 - par).start()

    slot, gate = slot_ref[...], gate_ref[...]
    lane = lax.broadcasted_iota(jnp.int32, (TT, WIN), 1)
    weights = jnp.concatenate([jnp.where(slot[:, e:e + 1] + off_ref[i * N_EXPERTS + e] == lane,
                                         gate[:, e:e + 1], 0.0).astype(BF16)
                               for e in range(N_EXPERTS)], axis=1)
    for e in range(N_EXPERTS):
        window_copy(i, e, par).wait()
    moe = jnp.dot(weights, ybuf[par], preferred_element_type=F32)
    n_extra = sum(jnp.maximum(off_ref[i * N_EXPERTS + e] + cnt_ref[i * N_EXPERTS + e] - 1, 0) // WIN
                  for e in range(N_EXPERTS))

    def finish(mixture):
        o_ref[...] = x1_ref[...] + ga2_ref[0] * (_rms(mixture) * g_ref[...])

    @pl.when(n_extra == 0)
    def _():
        finish(moe)

    def extra_windows(e, carry):
        k = i * N_EXPERTS + e

        def one(w, c):
            first = (w + 1) * WIN
            cp = pltpu.make_async_copy(
                ye_ref.at[e, pl.ds(pl.multiple_of(start_ref[k] + first, SEG_ALIGN), WIN), :], extra_ref, extra_sem)
            cp.start()
            pick = lax.broadcasted_iota(jnp.int32, (TT, N_EXPERTS), 1) == e
            slot_e = jnp.sum(jnp.where(pick, slot_ref[...], 0), axis=1, keepdims=True)
            gate_e = jnp.sum(jnp.where(pick, gate_ref[...], 0.0), axis=1, keepdims=True)
            wts = jnp.where(slot_e + (off_ref[k] - first) == lane, gate_e, 0.0).astype(BF16)
            cp.wait()
            acc_ref[...] += jnp.dot(wts, extra_ref[...], preferred_element_type=F32)
            return c

        return lax.fori_loop(0, jnp.maximum(off_ref[k] + cnt_ref[k] - 1, 0) // WIN, one, carry)

    @pl.when(n_extra > 0)
    def _():
        acc_ref[...] = moe
        lax.fori_loop(0, N_EXPERTS, extra_windows, 0)
        finish(acc_ref[...])


def _combine(start, off, cnt, slot, gate, x1, ga2, g_post2, ye, tiles_per_seq):
    n_tok, d = x1.shape
    tok = lambda w: pl.BlockSpec((TT, w), lambda i, *_: (i, 0))
    return pl.pallas_call(
        _combine_kernel,
        grid_spec=pltpu.PrefetchScalarGridSpec(
            num_scalar_prefetch=3,
            grid=(n_tok // TT,),
            in_specs=[tok(N_EXPERTS), tok(N_EXPERTS), tok(d),
                      pl.BlockSpec((1, 1, d), lambda i, *_: (i // tiles_per_seq, 0, 0)),
                      pl.BlockSpec((1, d), lambda i, *_: (0, 0)),
                      pl.BlockSpec(memory_space=pl.ANY)],
            out_specs=tok(d),
            scratch_shapes=[pltpu.VMEM((2, N_EXPERTS * WIN, d), BF16), pltpu.VMEM((WIN, d), BF16),
                            pltpu.VMEM((TT, d), F32),
                            pltpu.SemaphoreType.DMA((2,)), pltpu.SemaphoreType.DMA(())]),
        out_shape=jax.ShapeDtypeStruct((n_tok, d), F32),
        compiler_params=pltpu.CompilerParams(dimension_semantics=("arbitrary",), vmem_limit_bytes=VMEM_LIMIT),
        name="route_combine_final",
    )(start, off, cnt, slot, gate, x1, ga2, g_post2.reshape(1, d), ye)


def _moe(groups, g_post2, wg, wu, wd, tm_expert=512):
    d = D_MODEL
    routed = []
    used = 0
    for x1, h2, aff_t, ga2 in groups:
        b, s, _ = x1.shape
        n_tok = b * s
        cap = CAPACITY_FACTOR * n_tok // N_EXPERTS
        assert cap % SEG_ALIGN == 0
        slot_t, slot, gate, cnt = _select(aff_t, _thresholds(aff_t, cap))
        cnt = cnt[:, :, 0]
        base = jnp.cumsum(cnt, axis=0) - cnt + used
        start = base // SEG_ALIGN * SEG_ALIGN
        flat = lambda a: a.reshape(-1).astype(jnp.int32)
        routed.append((flat(start), flat(base - start), flat(cnt), slot_t, slot, gate))
        used += cap
    rows = -(-(used + WIN) // tm_expert) * tm_expert
    xe = jnp.zeros((N_EXPERTS, rows, d), BF16)
    for (x1, h2, _, _), (start, off, cnt, slot_t, _, _) in zip(groups, routed):
        xe = _dispatch(start, off, cnt, slot_t, h2.reshape(-1, d), xe)
    ye = _experts(jnp.full((N_EXPERTS,), used, jnp.int32), xe, wg, wu, wd, tm=tm_expert)
    outs = []
    for (x1, _, _, ga2), (start, off, cnt, _, slot, gate) in zip(groups, routed):
        b, s, _ = x1.shape
        outs.append(_combine(start, off, cnt, slot, gate, x1.reshape(-1, d), ga2, g_post2, ye, s // TT)
                    .reshape(b, s, d))
    return outs


def _hi_lo_rows(w):
    hi = w.astype(BF16)
    return jnp.concatenate([hi, (w - hi.astype(F32)).astype(BF16)], axis=0)


def _rope_tables(s):
    half = D_HEAD // 2
    inv = 1.0 / (ROPE_BASE ** (jnp.arange(half, dtype=F32) / half))
    ang = jnp.arange(s, dtype=F32)[:, None] * inv[None, :]
    cos, sin = jnp.cos(ang), jnp.sin(ang)
    return jnp.concatenate([cos, cos], axis=-1), jnp.concatenate([-sin, sin], axis=-1)


def _token_mixer(x, mod, p):
    b, s, d = x.shape
    sh1, sc1, ga1, sh2, sc2, ga2 = [m[:, None, :] for m in jnp.split(mod, 6, axis=-1)]
    cos2, sin2 = _rope_tables(s)
    u = _inproj(x, sc1, sh1, p["g_pre1"], cos2, sin2, p["w_in"])
    y_ret = _retention(u, p["log_gamma"], p["gn_ret"])
    y_hg = _hgrn2(u, p["lb"], p["gn_hg"], p["hg_consts"])
    x1, h2, aff_t = _merge(x, y_ret, y_hg, u, ga1, sc2, sh2, p["g_post1"], p["g_pre2"],
                           p["w_br_ret"], p["w_br_hg"], p["w_out"], p["w_router_t"])
    return x1, h2, aff_t, ga2


def kernel(x_prompt, x_sample, c_prompt, c_sample, w_ada, b_ada, g_pre1, g_post1, w_in, gn_ret, gn_hg, lb_logits, w_br_ret, w_br_hg, w_out, g_pre2, g_post2, w_router, w_gate, w_up, w_down):
    lower_bounds = jnp.cumsum(jax.nn.softmax(lb_logits.astype(F32), axis=0), axis=0)
    log_gamma = jnp.log(1.0 - jnp.power(2.0, -5.0 - jnp.arange(N_HEADS, dtype=F32)))
    consts = _hg_constants()
    n_gate = 2 * D_MODEL
    xp, xs = x_prompt, x_sample
    bp, bs = xp.shape[0], xs.shape[0]
    for l in range(w_ada.shape[0]):
        c_all = jnp.concatenate([c_prompt, c_sample, jnp.zeros((-(bp + bs) % 8, D_MODEL), F32)], axis=0)
        mod = _ada(c_all, w_ada[l], b_ada[l])
        w_in_l = w_in[l]
        p = dict(
            g_pre1=g_pre1[l], g_post1=g_post1[l], g_pre2=g_pre2[l], g_post2=g_post2[l],
            w_in=jnp.concatenate([w_in_l[:, -n_gate:], w_in_l[:, :-n_gate]], axis=1).astype(BF16),
            log_gamma=jnp.broadcast_to(log_gamma[:, None, None], (N_HEADS, 1, LANES)),
            gn_ret=gn_ret[l], gn_hg=gn_hg[l], lb=lower_bounds[l], hg_consts=consts,
            w_br_ret=w_br_ret[l].astype(BF16), w_br_hg=w_br_hg[l].astype(BF16), w_out=w_out[l].astype(BF16),
            w_router_t=_hi_lo_rows(w_router[l].T),
            w_gate=w_gate[l], w_up=w_up[l], w_down=w_down[l],
        )
        mixed = [_token_mixer(xp, mod[:bp], p), _token_mixer(xs, mod[bp:bp + bs], p)]
        xp, xs = _moe(mixed, p["g_post2"], p["w_gate"], p["w_up"], p["w_down"])
    return (xp, xs)
```

```python
import functools

import numpy as np
import jax
import jax.numpy as jnp
from jax import lax
from jax.experimental import pallas as pl
from jax.experimental.pallas import tpu as pltpu

F32 = jnp.float32
BF16 = jnp.bfloat16

D_MODEL = 1024
N_HEADS = 4
D_HEAD = 128
N_EXPERTS = 16
CAPACITY_FACTOR = 2
D_EXPERT = 2048
ROPE_BASE = 10000.0
NORM_EPS = 1e-6
CHUNK = 128
LANES = 128
VMEM_LIMIT = 56 * 1024 * 1024

D_IN = 2 * D_MODEL + 9 * N_HEADS * D_HEAD
COL_RETQ, COL_RETK, COL_RETV, COL_RETG = 16, 20, 24, 28
COL_HGQ, COL_HGI, COL_HGFF, COL_HGFB, COL_HGG = 32, 36, 40, 44, 48
TN_IN = 512
ROPE_CHUNKS = (COL_RETQ * LANES // TN_IN, COL_RETK * LANES // TN_IN)

LOG2_E = 1.4426950408889634
MIN_NORMAL = 2.0 ** -126
NT_DIMS = (((1,), (1,)), ((), ()))
TN_DIMS = (((0,), (0,)), ((), ()))


def _sigmoid(x):
    return 1.0 / (1.0 + jnp.exp(-x))


def _rms(x, eps=NORM_EPS):
    return x * lax.rsqrt(jnp.mean(x * x, axis=-1, keepdims=True) + eps)


def _ada_kernel(c_ref, w_ref, b_ref, o_ref):
    c = c_ref[...]
    o_ref[...] = jnp.dot(c * _sigmoid(c), w_ref[...], preferred_element_type=F32,
                         precision=lax.Precision.HIGHEST) + b_ref[...]


def _ada(c_all, w_ada, b_ada):
    rows = c_all.shape[0]
    n_out = w_ada.shape[1]
    tn = 1024
    return pl.pallas_call(
        _ada_kernel,
        grid=(n_out // tn,),
        in_specs=[pl.BlockSpec((rows, D_MODEL), lambda n: (0, 0)),
                  pl.BlockSpec((D_MODEL, tn), lambda n: (0, n)),
                  pl.BlockSpec((1, tn), lambda n: (0, n))],
        out_specs=pl.BlockSpec((rows, tn), lambda n: (0, n)),
        out_shape=jax.ShapeDtypeStruct((rows, n_out), F32),
        name="ada_mod",
    )(c_all, w_ada, b_ada.reshape(1, n_out))


def _inproj_kernel(x_ref, sc_ref, sh_ref, g_ref, cos_ref, sin_ref, w_ref, u_ref):
    h = (_rms(x_ref[0]) * (g_ref[...] * (1.0 + sc_ref[0])) + sh_ref[0]).astype(BF16)
    for n in range(D_IN // TN_IN):
        acc = jnp.dot(h, w_ref[:, n * TN_IN:(n + 1) * TN_IN], preferred_element_type=F32)
        if n in ROPE_CHUNKS:
            scale = 1.0 if n == ROPE_CHUNKS[0] else D_HEAD ** -0.5
            for hd in range(TN_IN // D_HEAD):
                blk = acc[:, hd * D_HEAD:(hd + 1) * D_HEAD]
                rot = blk * cos_ref[...] + pltpu.roll(blk, D_HEAD // 2, axis=1) * sin_ref[...]
                lo = n * TN_IN + hd * D_HEAD
                u_ref[0, :, lo:lo + D_HEAD] = (rot * scale).astype(BF16)
        else:
            u_ref[0, :, n * TN_IN:(n + 1) * TN_IN] = acc.astype(BF16)


def _inproj(x, sc, sh, g, cos2, sin2, w_bf16, tm=512):
    b, s, d = x.shape
    return pl.pallas_call(
        _inproj_kernel,
        grid=(b, s // tm),
        in_specs=[pl.BlockSpec((1, tm, d), lambda i, j: (i, j, 0)),
                  pl.BlockSpec((1, 1, d), lambda i, j: (i, 0, 0)),
                  pl.BlockSpec((1, 1, d), lambda i, j: (i, 0, 0)),
                  pl.BlockSpec((1, d), lambda i, j: (0, 0)),
                  pl.BlockSpec((tm, D_HEAD), lambda i, j: (j, 0)),
                  pl.BlockSpec((tm, D_HEAD), lambda i, j: (j, 0)),
                  pl.BlockSpec((d, D_IN), lambda i, j: (0, 0))],
        out_specs=pl.BlockSpec((1, tm, D_IN), lambda i, j: (i, j, 0)),
        out_shape=jax.ShapeDtypeStruct((b, s, D_IN), BF16),
        compiler_params=pltpu.CompilerParams(dimension_semantics=("parallel", "parallel"),
                                             vmem_limit_bytes=VMEM_LIMIT),
        name="inproj",
    )(x, sc, sh, g.reshape(1, d), cos2, sin2, w_bf16)


def _ret_kernel(q_ref, k_ref, v_ref, g_ref, lg_ref, gn_ref, o_ref, yf_ref, yb_ref, *, n_chunks):
    c = CHUNK
    lg = lg_ref[0]
    pos = lax.broadcasted_iota(jnp.int32, (c, LANES), 0).astype(F32)
    dist = jnp.abs(lax.broadcasted_iota(jnp.int32, (c, c), 0)
                   - lax.broadcasted_iota(jnp.int32, (c, c), 1)).astype(F32)
    decay = jnp.exp(lg * dist)
    q_fwd = jnp.exp(lg * (pos + 1.0)).astype(BF16)
    k_fwd = jnp.exp(lg * (c - 1.0 - pos)).astype(BF16)
    q_bwd = jnp.exp(lg * (c - pos)).astype(BF16)
    k_bwd = jnp.exp(lg * pos).astype(BF16)
    g_chunk = jnp.exp(lg * c)

    def scaled(t, f):
        return t * f

    def sweep(n, carry):
        r_f, r_b = carry
        sl = pl.ds(pl.multiple_of(n * c, c), c)
        q, k, v = q_ref[0, sl, :], k_ref[0, sl, :], v_ref[0, sl, :]
        s = lax.dot_general(q, k, NT_DIMS, preferred_element_type=F32) * decay
        lhs = jnp.concatenate([s.astype(BF16), scaled(q, q_fwd)], axis=1)
        rhs = jnp.concatenate([v, r_f.astype(BF16)], axis=0)
        yf_ref[sl, :] = jnp.dot(lhs, rhs, preferred_element_type=F32)
        r_f = g_chunk * r_f + lax.dot_general(scaled(k, k_fwd), v, TN_DIMS, preferred_element_type=F32)
        sl = pl.ds(pl.multiple_of((n_chunks - 1 - n) * c, c), c)
        q, k, v = q_ref[0, sl, :], k_ref[0, sl, :], v_ref[0, sl, :]
        yb_ref[sl, :] = jnp.dot(scaled(q, q_bwd), r_b.astype(BF16), preferred_element_type=F32)
        r_b = g_chunk * r_b + lax.dot_general(scaled(k, k_bwd), v, TN_DIMS, preferred_element_type=F32)
        return r_f, r_b

    zero = jnp.zeros((D_HEAD, D_HEAD), F32)
    lax.fori_loop(0, n_chunks, sweep, (zero, zero), unroll=16)

    def finish(n, carry):
        sl = pl.ds(pl.multiple_of(n * c, c), c)
        y = yf_ref[sl, :] + yb_ref[sl, :]
        yc = y - jnp.mean(y, axis=-1, keepdims=True)
        g = g_ref[0, sl, :].astype(F32)
        o_ref[0, sl, :] = (_rms(yc) * gn_ref[0] * (g * _sigmoid(g))).astype(BF16)
        return carry

    lax.fori_loop(0, n_chunks, finish, 0, unroll=4)


def _retention(u, log_gamma, gn):
    b, s, _ = u.shape

    def col(c0):
        return pl.BlockSpec((1, s, D_HEAD), lambda i, h: (i, 0, c0 + h))

    per_head = pl.BlockSpec((1, 1, D_HEAD), lambda i, h: (h, 0, 0))
    return pl.pallas_call(
        functools.partial(_ret_kernel, n_chunks=s // CHUNK),
        grid=(b, N_HEADS),
        in_specs=[col(COL_RETQ), col(COL_RETK), col(COL_RETV), col(COL_RETG), per_head, per_head],
        out_specs=pl.BlockSpec((1, s, D_HEAD), lambda i, h: (i, 0, h)),
        out_shape=jax.ShapeDtypeStruct((b, s, N_HEADS * D_HEAD), BF16),
        scratch_shapes=[pltpu.VMEM((s, D_HEAD), F32), pltpu.VMEM((s, D_HEAD), F32)],
        compiler_params=pltpu.CompilerParams(dimension_semantics=("parallel", "parallel"),
                                             vmem_limit_bytes=VMEM_LIMIT),
        name="retention",
    )(u, u, u, u, log_gamma, gn.reshape(N_HEADS, 1, D_HEAD))


N_LEVELS = 8
ROW_PIECE = 16
N_SPLIT_LEVEL = 5


def _hg_constants():
    c = CHUNK
    p = np.arange(c)
    tril = (p[None, :] <= p[:, None]).astype(np.float64)
    masks = [np.eye(c)]
    for l in range(1, N_LEVELS):
        s, half = 2 ** l, 2 ** (l - 1)
        blk = p // s
        upper = (p % s) >= half
        masks.append(((blk[:, None] == blk[None, :]) & upper[:, None] & (~upper)[None, :]).astype(np.float64))
    mask_f = np.stack(masks)
    mask_b = np.stack([m.T for m in masks])
    tile2 = lambda m: np.concatenate([m, m], axis=1)
    return (jnp.asarray(tile2(tril), BF16), jnp.asarray(tile2(tril[::-1, ::-1]), BF16),
            jnp.asarray(mask_f, BF16), jnp.asarray(mask_b, BF16))


def _neg_block_offsets(cum, level, forward):
    s = 2 ** level
    half = s // 2
    blocks = cum.reshape(CHUNK // s, s, LANES)
    lower, upper = blocks[:, :half, :], blocks[:, half:, :]
    if forward:
        ref = blocks[:, half - 1:half, :]
        out = jnp.concatenate([ref - lower, upper - ref], axis=1)
    else:
        ref = blocks[:, half:half + 1, :]
        out = jnp.concatenate([lower - ref, ref - upper], axis=1)
    return out.reshape(CHUNK, LANES)


SUBLANES = 8


def _neg_small_offsets(cum, lf, level, forward):
    s = 2 ** level
    half = s // 2
    row = lax.broadcasted_iota(jnp.int32, (CHUNK // SUBLANES, SUBLANES, LANES), 1)
    in_lower = (row & half) == 0
    if level == 1:
        keep = jnp.logical_not(in_lower) if forward else in_lower
        return jnp.where(keep.reshape(CHUNK, LANES), lf, 0.0)
    rows = cum.reshape(CHUNK // SUBLANES, SUBLANES, LANES)
    first = half - 1 if forward else half
    ref = rows[:, first:first + 1, :]
    for blk in range(1, SUBLANES // s):
        r = blk * s + first
        ref = jnp.where(row < blk * s, ref, rows[:, r:r + 1, :])
    diff = rows - ref
    return jnp.where(in_lower == forward, -diff, diff).reshape(CHUNK, LANES)


def _hg_gates(z, lb, mat_ref):
    sg = _sigmoid(z)
    lf = jnp.log(jnp.maximum(lb + (1.0 - lb) * sg, MIN_NORMAL)) * LOG2_E
    kb = ((1.0 - lb) * (1.0 - sg)).astype(BF16)
    hi = lf.astype(BF16)
    mid = (lf - hi.astype(F32)).astype(BF16)
    return kb, lf, jnp.dot(mat_ref[...], jnp.concatenate([hi, mid], axis=0), preferred_element_type=F32)


def _block_diag(a, b):
    zero = jnp.zeros_like(a)
    return jnp.concatenate([jnp.concatenate([a, zero], axis=1), jnp.concatenate([zero, b], axis=1)], axis=0)


def _hg_pair(fwd, bwd, lb, mats, masks):
    c = CHUNK
    dirs = (True, False)
    q = [fwd[0], bwd[0]]
    v = [fwd[1], bwd[1]]
    state = [fwd[3], bwd[3]]
    kb, lf, cum = zip(*[_hg_gates(d[2], lb, m) for d, m in zip((fwd, bwd), mats)])
    piece = lambda t, i: t[ROW_PIECE * i:ROW_PIECE * (i + 1)]
    n_pieces = c // ROW_PIECE
    halves = lambda p: (p[:, :c], p[:, c:])

    def paired_nt(lhs, rhs):
        return halves(lax.dot_general(jnp.concatenate(lhs, axis=1), _block_diag(*rhs), NT_DIMS,
                                      preferred_element_type=F32))

    def scale(l):
        neg = [_neg_small_offsets(cum[d], lf[d], l, dirs[d]) if 2 ** l <= SUBLANES
               else _neg_block_offsets(cum[d], l, dirs[d]) for d in range(2)]
        return [jnp.exp2(n).astype(BF16) for n in neg]

    a = [p.astype(BF16) * m[0] for p, m in zip(paired_nt(q, kb), masks)]
    for l in range(1, N_SPLIT_LEVEL):
        e = scale(l)
        p = paired_nt([q[d] * e[d] for d in range(2)], [kb[d] * e[d] for d in range(2)])
        a = [a[d] + p[d].astype(BF16) * masks[d][l] for d in range(2)]
    a = [[piece(a[d], i) for i in range(n_pieces)] for d in range(2)]
    for l in range(N_SPLIT_LEVEL, N_LEVELS):
        e = scale(l)
        per_half = 2 ** (l - 1) // ROW_PIECE
        is_query = [[((i // per_half) % 2 == 1) == dirs[d] for i in range(n_pieces)] for d in range(2)]
        queries = [[i for i in range(n_pieces) if is_query[d][i]] for d in range(2)]
        q_rows = [jnp.concatenate([piece(q[d], i) * piece(e[d], i) for i in queries[d]], axis=0) for d in range(2)]
        k_rows = [jnp.concatenate([piece(kb[d], i) if is_query[d][i] else piece(kb[d], i) * piece(e[d], i)
                                   for i in range(n_pieces)], axis=0) for d in range(2)]
        p = paired_nt(q_rows, k_rows)
        for d in range(2):
            for n, i in enumerate(queries[d]):
                a[d][i] = a[d][i] + (piece(p[d], n).astype(BF16)
                                     * masks[d][l, ROW_PIECE * i:ROW_PIECE * (i + 1), :])
    a = [jnp.concatenate(a[d], axis=0) for d in range(2)]
    y = halves(jnp.dot(jnp.concatenate(a, axis=1), _block_diag(*v), preferred_element_type=F32))
    carried = paired_nt([q[d] * jnp.exp2(cum[d]).astype(BF16) for d in range(2)],
                        [state[d].astype(BF16) for d in range(2)])
    out = []
    for d in range(2):
        total = cum[d][c - 1:c] if dirs[d] else cum[d][0:1]
        kd = kb[d] * jnp.exp2(total - cum[d]).astype(BF16)
        new_state = state[d] * jnp.exp2(total) + lax.dot_general(v[d], kd, TN_DIMS, preferred_element_type=F32)
        out.append((y[d] + carried[d], new_state))
    return out


def _hg_kernel(q_ref, i_ref, zf_ref, zb_ref, g_ref, lb_ref, gn_ref, matf_ref, matb_ref, maskf_ref, maskb_ref,
               o_ref, yf_ref, yb_ref, *, n_chunks):
    c = CHUNK
    lb = lb_ref[0]

    def sweep(n, carry):
        s_f, s_b = carry
        sf = pl.ds(pl.multiple_of(n * c, c), c)
        sb = pl.ds(pl.multiple_of((n_chunks - 1 - n) * c, c), c)
        (yf_ref[sf, :], s_f), (yb_ref[sb, :], s_b) = _hg_pair(
            (q_ref[0, sf, :], i_ref[0, sf, :], zf_ref[0, sf, :].astype(F32), s_f),
            (q_ref[0, sb, :], i_ref[0, sb, :], zb_ref[0, sb, :].astype(F32), s_b),
            lb, (matf_ref, matb_ref), (maskf_ref, maskb_ref))
        return s_f, s_b

    zero = jnp.zeros((D_HEAD, D_HEAD), F32)
    lax.fori_loop(0, n_chunks, sweep, (zero, zero), unroll=16)

    def finish(n, carry):
        sl = pl.ds(pl.multiple_of(n * c, c), c)
        g = g_ref[0, sl, :].astype(F32)
        o_ref[0, sl, :] = (_rms(yf_ref[sl, :] + yb_ref[sl, :]) * gn_ref[0] * (g * _sigmoid(g))).astype(BF16)
        return carry

    lax.fori_loop(0, n_chunks, finish, 0, unroll=4)


def _hgrn2(u, lb, gn, consts):
    b, s, _ = u.shape
    mat_f, mat_b, mask_f, mask_b = consts

    def col(c0):
        return pl.BlockSpec((1, s, D_HEAD), lambda i, h: (i, 0, c0 + h))

    def whole(a):
        return pl.BlockSpec(a.shape, lambda i, h: (0,) * a.ndim)

    per_head = pl.BlockSpec((1, 1, D_HEAD), lambda i, h: (h, 0, 0))
    return pl.pallas_call(
        functools.partial(_hg_kernel, n_chunks=s // CHUNK),
        grid=(b, N_HEADS),
        in_specs=[col(COL_HGQ), col(COL_HGI), col(COL_HGFF), col(COL_HGFB), col(COL_HGG), per_head, per_head,
                  whole(mat_f), whole(mat_b), whole(mask_f), whole(mask_b)],
        out_specs=pl.BlockSpec((1, s, D_HEAD), lambda i, h: (i, 0, h)),
        out_shape=jax.ShapeDtypeStruct((b, s, N_HEADS * D_HEAD), BF16),
        scratch_shapes=[pltpu.VMEM((s, D_HEAD), F32), pltpu.VMEM((s, D_HEAD), F32)],
        compiler_params=pltpu.CompilerParams(dimension_semantics=("parallel", "parallel"),
                                             vmem_limit_bytes=VMEM_LIMIT),
        name="hgrn2",
    )(u, u, u, u, u, lb.reshape(N_HEADS, 1, D_HEAD), gn.reshape(N_HEADS, 1, D_HEAD),
      mat_f, mat_b, mask_f, mask_b)


def _merge_kernel(x_ref, yr_ref, yh_ref, za_ref, zb_ref, ga1_ref, sc2_ref, sh2_ref, gpost1_ref, gpre2_ref,
                  wr_ref, wh_ref, wo_ref, wrt_ref, x1_ref, h2_ref, aff_ref):
    br = jnp.dot(yr_ref[0], wr_ref[...], preferred_element_type=F32)
    bh = jnp.dot(yh_ref[0], wh_ref[...], preferred_element_type=F32)
    merged = _sigmoid(za_ref[0].astype(F32)) * br + _sigmoid(zb_ref[0].astype(F32)) * bh
    o = jnp.dot(merged.astype(BF16), wo_ref[...], preferred_element_type=F32)
    x1 = x_ref[0] + _rms(o) * (ga1_ref[0] * gpost1_ref[...])
    x1_ref[0] = x1
    h2 = _rms(x1) * (gpre2_ref[...] * (1.0 + sc2_ref[0])) + sh2_ref[0]
    h2_hi = h2.astype(BF16)
    h2_ref[0] = h2_hi
    h2_lo = (h2 - h2_hi.astype(F32)).astype(BF16)
    w_parts = wrt_ref[...]
    by_hi = lax.dot_general(w_parts, h2_hi, NT_DIMS, preferred_element_type=F32)
    logits = (by_hi[:N_EXPERTS] + by_hi[N_EXPERTS:]
              + lax.dot_general(w_parts[:N_EXPERTS], h2_lo, NT_DIMS, preferred_element_type=F32))
    ex = jnp.exp(logits - jnp.max(logits, axis=0, keepdims=True))
    aff_ref[...] = ex / jnp.sum(ex, axis=0, keepdims=True)


def _merge(x, y_ret, y_hg, u, ga1, sc2, sh2, g_post1, g_pre2, w_br_ret, w_br_hg, w_out, w_router_t, tm=512):
    b, s, d = x.shape
    nj = s // tm
    tok = lambda w: pl.BlockSpec((1, tm, w), lambda i, j: (i, j, 0))
    per_b = pl.BlockSpec((1, 1, d), lambda i, j: (i, 0, 0))
    vec = pl.BlockSpec((1, d), lambda i, j: (0, 0))
    whole = lambda a: pl.BlockSpec(a.shape, lambda i, j: (0,) * a.ndim)
    return pl.pallas_call(
        _merge_kernel,
        grid=(b, nj),
        in_specs=[tok(d), tok(N_HEADS * D_HEAD), tok(N_HEADS * D_HEAD),
                  pl.BlockSpec((1, tm, d), lambda i, j: (i, j, 0)),
                  pl.BlockSpec((1, tm, d), lambda i, j: (i, j, 1)),
                  per_b, per_b, per_b, vec, vec,
                  whole(w_br_ret), whole(w_br_hg), whole(w_out), whole(w_router_t)],
        out_specs=[tok(d), tok(d), pl.BlockSpec((N_EXPERTS, tm), lambda i, j: (0, i * nj + j))],
        out_shape=[jax.ShapeDtypeStruct((b, s, d), F32), jax.ShapeDtypeStruct((b, s, d), BF16),
                   jax.ShapeDtypeStruct((N_EXPERTS, b * s), F32)],
        compiler_params=pltpu.CompilerParams(dimension_semantics=("parallel", "parallel"),
                                             vmem_limit_bytes=VMEM_LIMIT),
        name="merge_outproj_router",
    )(x, y_ret, y_hg, u, u, ga1, sc2, sh2, g_post1.reshape(1, d), g_pre2.reshape(1, d),
      w_br_ret, w_br_hg, w_out, w_router_t)


TF_EXPERT = 512


N_FF_CHUNKS = D_EXPERT // TF_EXPERT


def _expert_kernel(total_ref, x_ref, wg_ref, wu_ref, wd_ref, o_ref, wg_bf, wu_bf, wd_bf, *, tm, n_tiles):
    s, j = pl.program_id(0), pl.program_id(1)
    fill, use = s % 2, 1 - s % 2
    for c in range(3 * N_FF_CHUNKS):
        @pl.when(jnp.logical_and(j == c, s < pl.num_programs(0) - 1))
        def _():
            which, f = divmod(c, N_FF_CHUNKS)
            src, dst = ((wg_ref, wg_bf), (wu_ref, wu_bf), (wd_ref, wd_bf))[which]
            dst[fill, f] = src[0].astype(BF16)

    live_rows = jnp.where(jnp.logical_and(s > 0, j < n_tiles), total_ref[jnp.maximum(s - 1, 0)] - j * tm, 0)
    half = tm // 2

    def swiglu(x):
        acc = jnp.zeros((x.shape[0], D_MODEL), F32)
        for f in range(N_FF_CHUNKS):
            a = jnp.dot(x, wg_bf[use, f], preferred_element_type=F32)
            up = jnp.dot(x, wu_bf[use, f], preferred_element_type=F32)
            acc += jnp.dot((a * _sigmoid(a) * up).astype(BF16), wd_bf[use, f], preferred_element_type=F32)
        return acc.astype(BF16)

    @pl.when(live_rows > half)
    def _():
        o_ref[0] = swiglu(x_ref[0])

    @pl.when(jnp.logical_and(live_rows > 0, live_rows <= half))
    def _():
        o_ref[0, :half, :] = swiglu(x_ref[0, :half, :])
        o_ref[0, half:, :] = jnp.zeros((half, D_MODEL), BF16)

    @pl.when(jnp.logical_and(live_rows <= 0, j < n_tiles))
    def _():
        o_ref[0] = jnp.zeros((tm, D_MODEL), BF16)


def _experts(total, xe, wg, wu, wd, tm=512):
    e, rows, d = xe.shape
    n_tiles = rows // tm
    steps = max(n_tiles, 3 * N_FF_CHUNKS)
    nxt = lambda s: jnp.minimum(s, e - 1)
    cur = lambda s: jnp.maximum(s - 1, 0)
    tile = lambda j: jnp.minimum(j, n_tiles - 1)
    chunk = lambda j, which: jnp.clip(j - which * N_FF_CHUNKS, 0, N_FF_CHUNKS - 1)
    return pl.pallas_call(
        functools.partial(_expert_kernel, tm=tm, n_tiles=n_tiles),
        grid_spec=pltpu.PrefetchScalarGridSpec(
            num_scalar_prefetch=1,
            grid=(e + 1, steps),
            in_specs=[pl.BlockSpec((1, tm, d), lambda s, j, t: (cur(s), tile(j), 0)),
                      pl.BlockSpec((1, d, TF_EXPERT), lambda s, j, t: (nxt(s), 0, chunk(j, 0))),
                      pl.BlockSpec((1, d, TF_EXPERT), lambda s, j, t: (nxt(s), 0, chunk(j, 1))),
                      pl.BlockSpec((1, TF_EXPERT, d), lambda s, j, t: (nxt(s), chunk(j, 2), 0))],
            out_specs=pl.BlockSpec((1, tm, d), lambda s, j, t: (jnp.where(s == 0, e, s - 1), tile(j), 0)),
            scratch_shapes=[pltpu.VMEM((2, N_FF_CHUNKS, d, TF_EXPERT), BF16),
                            pltpu.VMEM((2, N_FF_CHUNKS, d, TF_EXPERT), BF16),
                            pltpu.VMEM((2, N_FF_CHUNKS, TF_EXPERT, d), BF16)]),
        out_shape=jax.ShapeDtypeStruct((e + 1, rows, d), BF16),
        compiler_params=pltpu.CompilerParams(dimension_semantics=("arbitrary", "arbitrary"),
                                             vmem_limit_bytes=VMEM_LIMIT),
        name="expert_ffn",
    )(total, xe, wg, wu, wd)


TT = 512
WIN = 128
SEG_ALIGN = 16
NOT_CHOSEN = -float(1 << 20)


MANTISSA_STEPS = 53


def _threshold_kernel(a_ref, o_ref, *, cap):
    def count_ge(x):
        return jnp.sum(jnp.where(a_ref[...] >= x, 1.0, 0.0), axis=1, keepdims=True)

    lo = jnp.where(count_ge(MIN_NORMAL) >= cap, MIN_NORMAL, 0.0) * jnp.ones((N_EXPERTS, 1), F32)
    for j in (64, 32, 16, 8, 4, 2, 1):
        cand = lo * (2.0 ** j)
        lo = jnp.where(count_ge(cand) >= cap, cand, lo)
    hi = jnp.where(lo > 0.0, 2.0 * lo, MIN_NORMAL)

    def halve(_, bounds):
        lo, hi = bounds
        mid = 0.5 * (lo + hi)
        ok = count_ge(mid) >= cap
        return jnp.where(ok, mid, lo), jnp.where(ok, hi, mid)

    lo, hi = lax.fori_loop(0, MANTISSA_STEPS, halve, (lo, hi))
    a = a_ref[...]
    thr = jnp.min(jnp.where(a >= lo, a, 2.0), axis=1, keepdims=True)
    n_gt = jnp.sum(jnp.where(a > thr, 1.0, 0.0), axis=1, keepdims=True)
    o_ref[0] = jnp.broadcast_to(thr, (N_EXPERTS, LANES))
    o_ref[1] = jnp.broadcast_to(cap - n_gt, (N_EXPERTS, LANES))


def _thresholds(aff_t, cap):
    return pl.pallas_call(
        functools.partial(_threshold_kernel, cap=cap),
        out_shape=jax.ShapeDtypeStruct((2, N_EXPERTS, LANES), F32),
        compiler_params=pltpu.CompilerParams(vmem_limit_bytes=VMEM_LIMIT),
        name="route_threshold",
    )(aff_t)


def _select_kernel(a_ref, thr_ref, upper_ref, slot_t_ref, slot_ref, gate_ref, cnt_ref, seen_ref):
    @pl.when(pl.program_id(0) == 0)
    def _():
        seen_ref[...] = jnp.zeros_like(seen_ref)

    a = a_ref[...]
    thr = thr_ref[0][:, 0:1]
    need = thr_ref[1][:, 0:1]
    gt, eq = a > thr, a == thr
    marks = jnp.concatenate([jnp.where(gt, 1.0, 0.0), jnp.where(eq, 1.0, 0.0)], axis=0).astype(BF16)
    before = jnp.dot(marks, upper_ref[...], preferred_element_type=F32)
    gt_before, eq_before = before[:N_EXPERTS], before[N_EXPERTS:]
    seen = seen_ref[:, 0:1]
    eq_rank = seen + eq_before
    sel = jnp.logical_or(gt, jnp.logical_and(eq, eq_rank < need))
    rank = gt_before + jnp.minimum(eq_rank, need) - jnp.minimum(seen, need)
    slot_t = jnp.where(sel, rank, NOT_CHOSEN)
    slot_t_ref[...] = slot_t.astype(jnp.int32)
    cnt_ref[0] = jnp.broadcast_to(jnp.sum(jnp.where(sel, 1, 0), axis=1, keepdims=True), (N_EXPERTS, LANES))
    seen_ref[...] = seen_ref[...] + jnp.sum(jnp.where(eq, 1.0, 0.0), axis=1, keepdims=True)
    pad = jnp.zeros((LANES - N_EXPERTS, TT), F32)
    slot_ref[...] = jnp.concatenate([slot_t, pad], axis=0).T[:, :N_EXPERTS].astype(jnp.int32)
    gate_ref[...] = jnp.concatenate([a, pad], axis=0).T[:, :N_EXPERTS]


def _select(aff_t, thr):
    n_tok = aff_t.shape[1]
    n_tiles = n_tok // TT
    t = np.arange(TT)
    upper = jnp.asarray(t[:, None] < t[None, :], BF16)
    return pl.pallas_call(
        _select_kernel,
        grid=(n_tiles,),
        in_specs=[pl.BlockSpec((N_EXPERTS, TT), lambda i: (0, i)),
                  pl.BlockSpec((2, N_EXPERTS, LANES), lambda i: (0, 0, 0)),
                  pl.BlockSpec((TT, TT), lambda i: (0, 0))],
        out_specs=[pl.BlockSpec((N_EXPERTS, TT), lambda i: (0, i)),
                   pl.BlockSpec((TT, N_EXPERTS), lambda i: (i, 0)),
                   pl.BlockSpec((TT, N_EXPERTS), lambda i: (i, 0)),
                   pl.BlockSpec((1, N_EXPERTS, LANES), lambda i: (i, 0, 0))],
        out_shape=[jax.ShapeDtypeStruct((N_EXPERTS, n_tok), jnp.int32),
                   jax.ShapeDtypeStruct((n_tok, N_EXPERTS), jnp.int32),
                   jax.ShapeDtypeStruct((n_tok, N_EXPERTS), F32),
                   jax.ShapeDtypeStruct((n_tiles, N_EXPERTS, LANES), jnp.int32)],
        scratch_shapes=[pltpu.VMEM((N_EXPERTS, LANES), F32)],
        compiler_params=pltpu.CompilerParams(dimension_semantics=("arbitrary",)),
        name="route_select",
    )(aff_t, thr, upper)


def _dispatch_kernel(start_ref, off_ref, cnt_ref, slot_ref, h_ref, xe_in_ref, xe_ref,
                     stage_ref, extra_ref, carry_ref, sem, extra_sem):
    del xe_in_ref
    i = pl.program_id(0)
    par = i % 2
    tiles_per_win = WIN // SEG_ALIGN

    @pl.when(i == 0)
    def _():
        carry_ref[...] = jnp.zeros_like(carry_ref)

    row = lax.broadcasted_iota(jnp.int32, (WIN, TT), 0)
    slot = slot_ref[...]
    onehot = jnp.concatenate([jnp.where(slot[e:e + 1, :] + off_ref[i * N_EXPERTS + e] == row, 1.0, 0.0).astype(BF16)
                              for e in range(N_EXPERTS)], axis=0)
    stage_ref[par] = jnp.dot(onehot, h_ref[...], preferred_element_type=F32).astype(BF16)

    def window_copy(e):
        start = pl.multiple_of(start_ref[i * N_EXPERTS + e], SEG_ALIGN)
        return pltpu.make_async_copy(stage_ref.at[par, pl.ds(e * WIN, WIN), :],
                                     xe_ref.at[e, pl.ds(start, WIN), :], sem.at[e])

    for e in range(N_EXPERTS):
        end = off_ref[i * N_EXPERTS + e] + cnt_ref[i * N_EXPERTS + e]
        head = pl.ds(e * WIN, SEG_ALIGN)
        stage_ref[par, head, :] = stage_ref[par, head, :] + carry_ref[e]
        tile = jnp.minimum(end // SEG_ALIGN, tiles_per_win - 1)
        last = stage_ref[par, pl.ds(pl.multiple_of(e * WIN + tile * SEG_ALIGN, SEG_ALIGN), SEG_ALIGN), :]
        carry_ref[e] = jnp.where(end < WIN, last, jnp.zeros_like(last))

        @pl.when(i > 0)
        def _():
            window_copy(e).wait()

        window_copy(e).start()

    def extra_windows(e, carry):
        k = i * N_EXPERTS + e
        end = off_ref[k] + cnt_ref[k]

        def one(w, c):
            first = (w + 1) * WIN
            srow = slot_ref[pl.ds(e, 1), :] + (off_ref[k] - first)
            oh = jnp.where(srow == row, 1.0, 0.0).astype(BF16)
            extra_ref[...] = jnp.dot(oh, h_ref[...], preferred_element_type=F32).astype(BF16)
            cp = pltpu.make_async_copy(
                extra_ref, xe_ref.at[e, pl.ds(pl.multiple_of(start_ref[k] + first, SEG_ALIGN), WIN), :], extra_sem)
            cp.start()
            tile = end // SEG_ALIGN - (w + 1) * tiles_per_win
            inside = jnp.logical_and(tile >= 0, tile < tiles_per_win)
            last = extra_ref[pl.ds(pl.multiple_of(jnp.clip(tile, 0, tiles_per_win - 1) * SEG_ALIGN, SEG_ALIGN),
                                   SEG_ALIGN), :]
            carry_ref[e] = jnp.where(inside, last, carry_ref[e])
            cp.wait()
            return c

        return lax.fori_loop(0, jnp.maximum(end - 1, 0) // WIN, one, carry)

    lax.fori_loop(0, N_EXPERTS, extra_windows, 0)

    @pl.when(i == pl.num_programs(0) - 1)
    def _():
        for e in range(N_EXPERTS):
            window_copy(e).wait()


def _dispatch(start, off, cnt, slot_t, h2, xe0):
    n_tok, d = h2.shape
    rows = xe0.shape[1]
    return pl.pallas_call(
        _dispatch_kernel,
        grid_spec=pltpu.PrefetchScalarGridSpec(
            num_scalar_prefetch=3,
            grid=(n_tok // TT,),
            in_specs=[pl.BlockSpec((N_EXPERTS, TT), lambda i, *_: (0, i)),
                      pl.BlockSpec((TT, d), lambda i, *_: (i, 0)),
                      pl.BlockSpec(memory_space=pl.ANY)],
            out_specs=pl.BlockSpec(memory_space=pl.ANY),
            scratch_shapes=[pltpu.VMEM((2, N_EXPERTS * WIN, d), BF16), pltpu.VMEM((WIN, d), BF16),
                            pltpu.VMEM((N_EXPERTS, SEG_ALIGN, d), BF16),
                            pltpu.SemaphoreType.DMA((N_EXPERTS,)), pltpu.SemaphoreType.DMA(())]),
        out_shape=jax.ShapeDtypeStruct((N_EXPERTS, rows, d), BF16),
        input_output_aliases={5: 0},
        compiler_params=pltpu.CompilerParams(dimension_semantics=("arbitrary",), vmem_limit_bytes=VMEM_LIMIT),
        name="route_dispatch",
    )(start, off, cnt, slot_t, h2, xe0)


def _combine_kernel(start_ref, off_ref, cnt_ref, slot_ref, gate_ref, x1_ref, ga2_ref, g_ref, ye_ref, o_ref,
                    ybuf, extra_ref, acc_ref, sem, extra_sem):
    i = pl.program_id(0)
    par = i % 2

    def window_copy(step, e, buf):
        start = pl.multiple_of(start_ref[step * N_EXPERTS + e], SEG_ALIGN)
        return pltpu.make_async_copy(ye_ref.at[e, pl.ds(start, WIN), :],
                                     ybuf.at[buf, pl.ds(e * WIN, WIN), :], sem.at[buf])

    @pl.when(i == 0)
    def _():
        for e in range(N_EXPERTS):
            window_copy(0, e, 0).start()

    @pl.when(i + 1 < pl.num_programs(0))
    def _():
        for e in range(N_EXPERTS):
            window_copy(i + 1, e, 1 - par).start()

    slot, gate = slot_ref[...], gate_ref[...]
    lane = lax.broadcasted_iota(jnp.int32, (TT, WIN), 1)
    weights = jnp.concatenate([jnp.where(slot[:, e:e + 1] + off_ref[i * N_EXPERTS + e] == lane,
                                         gate[:, e:e + 1], 0.0).astype(BF16)
                               for e in range(N_EXPERTS)], axis=1)
    for e in range(N_EXPERTS):
        window_copy(i, e, par).wait()
    moe = jnp.dot(weights, ybuf[par], preferred_element_type=F32)
    n_extra = sum(jnp.maximum(off_ref[i * N_EXPERTS + e] + cnt_ref[i * N_EXPERTS + e] - 1, 0) // WIN
                  for e in range(N_EXPERTS))

    def finish(mixture):
        o_ref[...] = x1_ref[...] + ga2_ref[0] * (_rms(mixture) * g_ref[...])

    @pl.when(n_extra == 0)
    def _():
        finish(moe)

    def extra_windows(e, carry):
        k = i * N_EXPERTS + e

        def one(w, c):
            first = (w + 1) * WIN
            cp = pltpu.make_async_copy(
                ye_ref.at[e, pl.ds(pl.multiple_of(start_ref[k] + first, SEG_ALIGN), WIN), :], extra_ref, extra_sem)
            cp.start()
            pick = lax.broadcasted_iota(jnp.int32, (TT, N_EXPERTS), 1) == e
            slot_e = jnp.sum(jnp.where(pick, slot_ref[...], 0), axis=1, keepdims=True)
            gate_e = jnp.sum(jnp.where(pick, gate_ref[...], 0.0), axis=1, keepdims=True)
            wts = jnp.where(slot_e + (off_ref[k] - first) == lane, gate_e, 0.0).astype(BF16)
            cp.wait()
            acc_ref[...] += jnp.dot(wts, extra_ref[...], preferred_element_type=F32)
            return c

        return lax.fori_loop(0, jnp.maximum(off_ref[k] + cnt_ref[k] - 1, 0) // WIN, one, carry)

    @pl.when(n_extra > 0)
    def _():
        acc_ref[...] = moe
        lax.fori_loop(0, N_EXPERTS, extra_windows, 0)
        finish(acc_ref[...])


def _combine(start, off, cnt, slot, gate, x1, ga2, g_post2, ye, tiles_per_seq):
    n_tok, d = x1.shape
    tok = lambda w: pl.BlockSpec((TT, w), lambda i, *_: (i, 0))
    return pl.pallas_call(
        _combine_kernel,
        grid_spec=pltpu.PrefetchScalarGridSpec(
            num_scalar_prefetch=3,
            grid=(n_tok // TT,),
            in_specs=[tok(N_EXPERTS), tok(N_EXPERTS), tok(d),
                      pl.BlockSpec((1, 1, d), lambda i, *_: (i // tiles_per_seq, 0, 0)),
                      pl.BlockSpec((1, d), lambda i, *_: (0, 0)),
                      pl.BlockSpec(memory_space=pl.ANY)],
            out_specs=tok(d),
            scratch_shapes=[pltpu.VMEM((2, N_EXPERTS * WIN, d), BF16), pltpu.VMEM((WIN, d), BF16),
                            pltpu.VMEM((TT, d), F32),
                            pltpu.SemaphoreType.DMA((2,)), pltpu.SemaphoreType.DMA(())]),
        out_shape=jax.ShapeDtypeStruct((n_tok, d), F32),
        compiler_params=pltpu.CompilerParams(dimension_semantics=("arbitrary",), vmem_limit_bytes=VMEM_LIMIT),
        name="route_combine_final",
    )(start, off, cnt, slot, gate, x1, ga2, g_post2.reshape(1, d), ye)


def _moe(groups, g_post2, wg, wu, wd, tm_expert=512):
    d = D_MODEL
    routed = []
    used = 0
    for x1, h2, aff_t, ga2 in groups:
        b, s, _ = x1.shape
        n_tok = b * s
        cap = CAPACITY_FACTOR * n_tok // N_EXPERTS
        assert cap % SEG_ALIGN == 0
        slot_t, slot, gate, cnt = _select(aff_t, _thresholds(aff_t, cap))
        cnt = cnt[:, :, 0]
        base = jnp.cumsum(cnt, axis=0) - cnt + used
        start = base // SEG_ALIGN * SEG_ALIGN
        flat = lambda a: a.reshape(-1).astype(jnp.int32)
        routed.append((flat(start), flat(base - start), flat(cnt), slot_t, slot, gate))
        used += cap
    rows = -(-(used + WIN) // tm_expert) * tm_expert
    xe = jnp.zeros((N_EXPERTS, rows, d), BF16)
    for (x1, h2, _, _), (start, off, cnt, slot_t, _, _) in zip(groups, routed):
        xe = _dispatch(start, off, cnt, slot_t, h2.reshape(-1, d), xe)
    ye = _experts(jnp.full((N_EXPERTS,), used, jnp.int32), xe, wg, wu, wd, tm=tm_expert)
    outs = []
    for (x1, _, _, ga2), (start, off, cnt, _, slot, gate) in zip(groups, routed):
        b, s, _ = x1.shape
        outs.append(_combine(start, off, cnt, slot, gate, x1.reshape(-1, d), ga2, g_post2, ye, s // TT)
                    .reshape(b, s, d))
    return outs


def _hi_lo_rows(w):
    hi = w.astype(BF16)
    return jnp.concatenate([hi, (w - hi.astype(F32)).astype(BF16)], axis=0)


def _rope_tables(s):
    half = D_HEAD // 2
    inv = 1.0 / (ROPE_BASE ** (jnp.arange(half, dtype=F32) / half))
    ang = jnp.arange(s, dtype=F32)[:, None] * inv[None, :]
    cos, sin = jnp.cos(ang), jnp.sin(ang)
    return jnp.concatenate([cos, cos], axis=-1), jnp.concatenate([-sin, sin], axis=-1)


def _token_mixer(x, mod, p):
    b, s, d = x.shape
    sh1, sc1, ga1, sh2, sc2, ga2 = [m[:, None, :] for m in jnp.split(mod, 6, axis=-1)]
    cos2, sin2 = _rope_tables(s)
    u = _inproj(x, sc1, sh1, p["g_pre1"], cos2, sin2, p["w_in"])
    y_ret = _retention(u, p["log_gamma"], p["gn_ret"])
    y_hg = _hgrn2(u, p["lb"], p["gn_hg"], p["hg_consts"])
    x1, h2, aff_t = _merge(x, y_ret, y_hg, u, ga1, sc2, sh2, p["g_post1"], p["g_pre2"],
                           p["w_br_ret"], p["w_br_hg"], p["w_out"], p["w_router_t"])
    return x1, h2, aff_t, ga2


def kernel(x_prompt, x_sample, c_prompt, c_sample, w_ada, b_ada, g_pre1, g_post1, w_in, gn_ret, gn_hg, lb_logits, w_br_ret, w_br_hg, w_out, g_pre2, g_post2, w_router, w_gate, w_up, w_down):
    lower_bounds = jnp.cumsum(jax.nn.softmax(lb_logits.astype(F32), axis=0), axis=0)
    log_gamma = jnp.log(1.0 - jnp.power(2.0, -5.0 - jnp.arange(N_HEADS, dtype=F32)))
    consts = _hg_constants()
    n_gate = 2 * D_MODEL
    xp, xs = x_prompt, x_sample
    bp, bs = xp.shape[0], xs.shape[0]
    for l in range(w_ada.shape[0]):
        c_all = jnp.concatenate([c_prompt, c_sample, jnp.zeros((-(bp + bs) % 8, D_MODEL), F32)], axis=0)
        mod = _ada(c_all, w_ada[l], b_ada[l])
        w_in_l = w_in[l]
        p = dict(
            g_pre1=g_pre1[l], g_post1=g_post1[l], g_pre2=g_pre2[l], g_post2=g_post2[l],
            w_in=jnp.concatenate([w_in_l[:, -n_gate:], w_in_l[:, :-n_gate]], axis=1).astype(BF16),
            log_gamma=jnp.broadcast_to(log_gamma[:, None, None], (N_HEADS, 1, LANES)),
            gn_ret=gn_ret[l], gn_hg=gn_hg[l], lb=lower_bounds[l], hg_consts=consts,
            w_br_ret=w_br_ret[l].astype(BF16), w_br_hg=w_br_hg[l].astype(BF16), w_out=w_out[l].astype(BF16),
            w_router_t=_hi_lo_rows(w_router[l].T),
            w_gate=w_gate[l], w_up=w_up[l], w_down=w_down[l],
        )
        mixed = [_token_mixer(xp, mod[:bp], p), _token_mixer(xs, mod[bp:bp + bs], p)]
        xp, xs = _moe(mixed, p["g_post2"], p["w_gate"], p["w_up"], p["w_down"])
    return (xp, xs)
```

```python
import functools

import numpy as np
import jax
import jax.numpy as jnp
from jax import lax
from jax.experimental import pallas as pl
from jax.experimental.pallas import tpu as pltpu

F32 = jnp.float32
BF16 = jnp.bfloat16

D_MODEL = 1024
N_HEADS = 4
D_HEAD = 128
N_EXPERTS = 16
CAPACITY_FACTOR = 2
D_EXPERT = 2048
ROPE_BASE = 10000.0
NORM_EPS = 1e-6
CHUNK = 128
SWEEP_UNROLL = 16
LANES = 128
VMEM_LIMIT = 56 * 1024 * 1024

D_IN = 2 * D_MODEL + 9 * N_HEADS * D_HEAD
COL_RETQ, COL_RETK, COL_RETV, COL_RETG = 16, 20, 24, 28
COL_HGQ, COL_HGI, COL_HGFF, COL_HGFB, COL_HGG = 32, 36, 40, 44, 48
TN_IN = 512
ROPE_CHUNKS = (COL_RETQ * LANES // TN_IN, COL_RETK * LANES // TN_IN)

LOG2_E = 1.4426950408889634
MIN_NORMAL = 2.0 ** -126
NT_DIMS = (((1,), (1,)), ((), ()))
TN_DIMS = (((0,), (0,)), ((), ()))


def _sigmoid(x):
    return 1.0 / (1.0 + jnp.exp(-x))


def _rms(x, eps=NORM_EPS):
    return x * lax.rsqrt(jnp.mean(x * x, axis=-1, keepdims=True) + eps)


def _ada_kernel(c_ref, w_ref, b_ref, o_ref):
    c = c_ref[...]
    o_ref[...] = jnp.dot(c * _sigmoid(c), w_ref[...], preferred_element_type=F32,
                         precision=lax.Precision.HIGHEST) + b_ref[...]


def _ada(c_all, w_ada, b_ada):
    rows = c_all.shape[0]
    n_out = w_ada.shape[1]
    tn = 1024
    return pl.pallas_call(
        _ada_kernel,
        grid=(n_out // tn,),
        in_specs=[pl.BlockSpec((rows, D_MODEL), lambda n: (0, 0)),
                  pl.BlockSpec((D_MODEL, tn), lambda n: (0, n)),
                  pl.BlockSpec((1, tn), lambda n: (0, n))],
        out_specs=pl.BlockSpec((rows, tn), lambda n: (0, n)),
        out_shape=jax.ShapeDtypeStruct((rows, n_out), F32),
        name="ada_mod",
    )(c_all, w_ada, b_ada.reshape(1, n_out))


def _inproj_kernel(x_ref, sc_ref, sh_ref, g_ref, cos_ref, sin_ref, w_ref, u_ref):
    h = (_rms(x_ref[0]) * (g_ref[...] * (1.0 + sc_ref[0])) + sh_ref[0]).astype(BF16)
    for n in range(D_IN // TN_IN):
        acc = jnp.dot(h, w_ref[:, n * TN_IN:(n + 1) * TN_IN], preferred_element_type=F32)
        if n in ROPE_CHUNKS:
            scale = 1.0 if n == ROPE_CHUNKS[0] else D_HEAD ** -0.5
            for hd in range(TN_IN // D_HEAD):
                blk = acc[:, hd * D_HEAD:(hd + 1) * D_HEAD]
                rot = blk * cos_ref[...] + pltpu.roll(blk, D_HEAD // 2, axis=1) * sin_ref[...]
                lo = n * TN_IN + hd * D_HEAD
                u_ref[0, :, lo:lo + D_HEAD] = (rot * scale).astype(BF16)
        else:
            u_ref[0, :, n * TN_IN:(n + 1) * TN_IN] = acc.astype(BF16)


def _inproj(x, sc, sh, g, cos2, sin2, w_bf16, tm=512):
    b, s, d = x.shape
    return pl.pallas_call(
        _inproj_kernel,
        grid=(b, s // tm),
        in_specs=[pl.BlockSpec((1, tm, d), lambda i, j: (i, j, 0)),
                  pl.BlockSpec((1, 1, d), lambda i, j: (i, 0, 0)),
                  pl.BlockSpec((1, 1, d), lambda i, j: (i, 0, 0)),
                  pl.BlockSpec((1, d), lambda i, j: (0, 0)),
                  pl.BlockSpec((tm, D_HEAD), lambda i, j: (j, 0)),
                  pl.BlockSpec((tm, D_HEAD), lambda i, j: (j, 0)),
                  pl.BlockSpec((d, D_IN), lambda i, j: (0, 0))],
        out_specs=pl.BlockSpec((1, tm, D_IN), lambda i, j: (i, j, 0)),
        out_shape=jax.ShapeDtypeStruct((b, s, D_IN), BF16),
        compiler_params=pltpu.CompilerParams(dimension_semantics=("parallel", "parallel"),
                                             vmem_limit_bytes=VMEM_LIMIT),
        name="inproj",
    )(x, sc, sh, g.reshape(1, d), cos2, sin2, w_bf16)


def _ret_kernel(q_ref, k_ref, v_ref, g_ref, lg_ref, gn_ref, o_ref, yf_ref, yb_ref, *, n_chunks):
    c = CHUNK
    lg = lg_ref[0]
    pos = lax.broadcasted_iota(jnp.int32, (c, LANES), 0).astype(F32)
    dist = jnp.abs(lax.broadcasted_iota(jnp.int32, (c, c), 0)
                   - lax.broadcasted_iota(jnp.int32, (c, c), 1)).astype(F32)
    decay = jnp.exp(lg * dist)
    q_fwd = jnp.exp(lg * (pos + 1.0)).astype(BF16)
    k_fwd = jnp.exp(lg * (c - 1.0 - pos)).astype(BF16)
    q_bwd = jnp.exp(lg * (c - pos)).astype(BF16)
    k_bwd = jnp.exp(lg * pos).astype(BF16)
    g_chunk = jnp.exp(lg * c)

    def scaled(t, f):
        return t * f

    def finish(sl, y):
        yc = y - jnp.mean(y, axis=-1, keepdims=True)
        g = g_ref[0, sl, :].astype(F32)
        o_ref[0, sl, :] = (_rms(yc) * gn_ref[0] * (g * _sigmoid(g))).astype(BF16)

    def sweep(n, carry, *, finishing):
        r_f, r_b = carry
        sf = pl.ds(pl.multiple_of(n * c, c), c)
        sb = pl.ds(pl.multiple_of((n_chunks - 1 - n) * c, c), c)
        q, k, v = q_ref[0, sf, :], k_ref[0, sf, :], v_ref[0, sf, :]
        s = lax.dot_general(q, k, NT_DIMS, preferred_element_type=F32) * decay
        lhs = jnp.concatenate([s.astype(BF16), scaled(q, q_fwd)], axis=1)
        rhs = jnp.concatenate([v, r_f.astype(BF16)], axis=0)
        y_f = jnp.dot(lhs, rhs, preferred_element_type=F32)
        r_f = g_chunk * r_f + lax.dot_general(scaled(k, k_fwd), v, TN_DIMS, preferred_element_type=F32)
        q, k, v = q_ref[0, sb, :], k_ref[0, sb, :], v_ref[0, sb, :]
        y_b = jnp.dot(scaled(q, q_bwd), r_b.astype(BF16), preferred_element_type=F32)
        r_b = g_chunk * r_b + lax.dot_general(scaled(k, k_bwd), v, TN_DIMS, preferred_element_type=F32)
        if finishing:
            finish(sf, y_f + yb_ref[sf, :])
            finish(sb, y_b + yf_ref[sb, :])
        else:
            yf_ref[sf, :] = y_f
            yb_ref[sb, :] = y_b
        return r_f, r_b

    zero = jnp.zeros((D_HEAD, D_HEAD), F32)
    half = n_chunks // 2
    unroll = min(SWEEP_UNROLL, half)
    carry = lax.fori_loop(0, half, functools.partial(sweep, finishing=False), (zero, zero), unroll=unroll)
    lax.fori_loop(half, n_chunks, functools.partial(sweep, finishing=True), carry, unroll=unroll)


def _retention(u, log_gamma, gn):
    b, s, _ = u.shape

    def col(c0):
        return pl.BlockSpec((1, s, D_HEAD), lambda i, h: (i, 0, c0 + h))

    per_head = pl.BlockSpec((1, 1, D_HEAD), lambda i, h: (h, 0, 0))
    return pl.pallas_call(
        functools.partial(_ret_kernel, n_chunks=s // CHUNK),
        grid=(b, N_HEADS),
        in_specs=[col(COL_RETQ), col(COL_RETK), col(COL_RETV), col(COL_RETG), per_head, per_head],
        out_specs=pl.BlockSpec((1, s, D_HEAD), lambda i, h: (i, 0, h)),
        out_shape=jax.ShapeDtypeStruct((b, s, N_HEADS * D_HEAD), BF16),
        scratch_shapes=[pltpu.VMEM((s, D_HEAD), F32), pltpu.VMEM((s, D_HEAD), F32)],
        compiler_params=pltpu.CompilerParams(dimension_semantics=("parallel", "parallel"),
                                             vmem_limit_bytes=VMEM_LIMIT),
        name="retention",
    )(u, u, u, u, log_gamma, gn.reshape(N_HEADS, 1, D_HEAD))


N_LEVELS = 8
ROW_PIECE = 16
N_SPLIT_LEVEL = 5


def _hg_constants():
    c = CHUNK
    p = np.arange(c)
    tril = (p[None, :] <= p[:, None]).astype(np.float64)
    masks = [np.eye(c)]
    for l in range(1, N_LEVELS):
        s, half = 2 ** l, 2 ** (l - 1)
        blk = p // s
        upper = (p % s) >= half
        masks.append(((blk[:, None] == blk[None, :]) & upper[:, None] & (~upper)[None, :]).astype(np.float64))
    mask_f = np.stack(masks)
    mask_b = np.stack([m.T for m in masks])
    tile2 = lambda m: np.concatenate([m, m], axis=1)
    return (jnp.asarray(tile2(tril), BF16), jnp.asarray(tile2(tril[::-1, ::-1]), BF16),
            jnp.asarray(mask_f, BF16), jnp.asarray(mask_b, BF16))


def _neg_block_offsets(cum, level, forward):
    s = 2 ** level
    half = s // 2
    blocks = cum.reshape(CHUNK // s, s, LANES)
    lower, upper = blocks[:, :half, :], blocks[:, half:, :]
    if forward:
        ref = blocks[:, half - 1:half, :]
        out = jnp.concatenate([ref - lower, upper - ref], axis=1)
    else:
        ref = blocks[:, half:half + 1, :]
        out = jnp.concatenate([lower - ref, ref - upper], axis=1)
    return out.reshape(CHUNK, LANES)


SUBLANES = 8


def _neg_small_offsets(cum, lf, level, forward):
    s = 2 ** level
    half = s // 2
    row = lax.broadcasted_iota(jnp.int32, (CHUNK // SUBLANES, SUBLANES, LANES), 1)
    in_lower = (row & half) == 0
    if level == 1:
        keep = jnp.logical_not(in_lower) if forward else in_lower
        return jnp.where(keep.reshape(CHUNK, LANES), lf, 0.0)
    rows = cum.reshape(CHUNK // SUBLANES, SUBLANES, LANES)
    first = half - 1 if forward else half
    ref = rows[:, first:first + 1, :]
    for blk in range(1, SUBLANES // s):
        r = blk * s + first
        ref = jnp.where(row < blk * s, ref, rows[:, r:r + 1, :])
    diff = rows - ref
    return jnp.where(in_lower == forward, -diff, diff).reshape(CHUNK, LANES)


def _hg_gates(z, lb, mat_ref):
    sg = _sigmoid(z)
    lf = jnp.log(jnp.maximum(lb + (1.0 - lb) * sg, MIN_NORMAL)) * LOG2_E
    kb = ((1.0 - lb) * (1.0 - sg)).astype(BF16)
    hi = lf.astype(BF16)
    mid = (lf - hi.astype(F32)).astype(BF16)
    return kb, lf, jnp.dot(mat_ref[...], jnp.concatenate([hi, mid], axis=0), preferred_element_type=F32)


def _block_diag(a, b):
    zero = jnp.zeros_like(a)
    return jnp.concatenate([jnp.concatenate([a, zero], axis=1), jnp.concatenate([zero, b], axis=1)], axis=0)


def _hg_pair(fwd, bwd, lb, mats, masks):
    c = CHUNK
    dirs = (True, False)
    q = [fwd[0], bwd[0]]
    v = [fwd[1], bwd[1]]
    state = [fwd[3], bwd[3]]
    kb, lf, cum = zip(*[_hg_gates(d[2], lb, m) for d, m in zip((fwd, bwd), mats)])
    piece = lambda t, i: t[ROW_PIECE * i:ROW_PIECE * (i + 1)]
    n_pieces = c // ROW_PIECE
    halves = lambda p: (p[:, :c], p[:, c:])

    def paired_nt(lhs, rhs):
        return halves(lax.dot_general(jnp.concatenate(lhs, axis=1), _block_diag(*rhs), NT_DIMS,
                                      preferred_element_type=F32))

    def scale(l):
        neg = [_neg_small_offsets(cum[d], lf[d], l, dirs[d]) if 2 ** l <= SUBLANES
               else _neg_block_offsets(cum[d], l, dirs[d]) for d in range(2)]
        return [jnp.exp2(n).astype(BF16) for n in neg]

    a = [p.astype(BF16) * m[0] for p, m in zip(paired_nt(q, kb), masks)]
    for l in range(1, N_SPLIT_LEVEL):
        e = scale(l)
        p = paired_nt([q[d] * e[d] for d in range(2)], [kb[d] * e[d] for d in range(2)])
        a = [a[d] + p[d].astype(BF16) * masks[d][l] for d in range(2)]
    a = [[piece(a[d], i) for i in range(n_pieces)] for d in range(2)]
    for l in range(N_SPLIT_LEVEL, N_LEVELS):
        e = scale(l)
        per_half = 2 ** (l - 1) // ROW_PIECE
        is_query = [[((i // per_half) % 2 == 1) == dirs[d] for i in range(n_pieces)] for d in range(2)]
        queries = [[i for i in range(n_pieces) if is_query[d][i]] for d in range(2)]
        q_rows = [jnp.concatenate([piece(q[d], i) * piece(e[d], i) for i in queries[d]], axis=0) for d in range(2)]
        k_rows = [jnp.concatenate([piece(kb[d], i) if is_query[d][i] else piece(kb[d], i) * piece(e[d], i)
                                   for i in range(n_pieces)], axis=0) for d in range(2)]
        p = paired_nt(q_rows, k_rows)
        for d in range(2):
            for n, i in enumerate(queries[d]):
                a[d][i] = a[d][i] + (piece(p[d], n).astype(BF16)
                                     * masks[d][l, ROW_PIECE * i:ROW_PIECE * (i + 1), :])
    a = [jnp.concatenate(a[d], axis=0) for d in range(2)]
    y = halves(jnp.dot(jnp.concatenate(a, axis=1), _block_diag(*v), preferred_element_type=F32))
    carried = paired_nt([q[d] * jnp.exp2(cum[d]).astype(BF16) for d in range(2)],
                        [state[d].astype(BF16) for d in range(2)])
    out = []
    for d in range(2):
        total = cum[d][c - 1:c] if dirs[d] else cum[d][0:1]
        kd = kb[d] * jnp.exp2(total - cum[d]).astype(BF16)
        new_state = state[d] * jnp.exp2(total) + lax.dot_general(v[d], kd, TN_DIMS, preferred_element_type=F32)
        out.append((y[d] + carried[d], new_state))
    return out


def _hg_kernel(q_ref, i_ref, zf_ref, zb_ref, g_ref, lb_ref, gn_ref, matf_ref, matb_ref, maskf_ref, maskb_ref,
               o_ref, yf_ref, yb_ref, *, n_chunks):
    c = CHUNK
    lb = lb_ref[0]

    def finish(sl, y):
        g = g_ref[0, sl, :].astype(F32)
        o_ref[0, sl, :] = (_rms(y) * gn_ref[0] * (g * _sigmoid(g))).astype(BF16)

    def sweep(n, carry, *, finishing):
        s_f, s_b = carry
        sf = pl.ds(pl.multiple_of(n * c, c), c)
        sb = pl.ds(pl.multiple_of((n_chunks - 1 - n) * c, c), c)
        (y_f, s_f), (y_b, s_b) = _hg_pair(
            (q_ref[0, sf, :], i_ref[0, sf, :], zf_ref[0, sf, :].astype(F32), s_f),
            (q_ref[0, sb, :], i_ref[0, sb, :], zb_ref[0, sb, :].astype(F32), s_b),
            lb, (matf_ref, matb_ref), (maskf_ref, maskb_ref))
        if finishing:
            finish(sf, y_f + yb_ref[sf, :])
            finish(sb, y_b + yf_ref[sb, :])
        else:
            yf_ref[sf, :] = y_f
            yb_ref[sb, :] = y_b
        return s_f, s_b

    zero = jnp.zeros((D_HEAD, D_HEAD), F32)
    half = n_chunks // 2
    unroll = min(SWEEP_UNROLL, half)
    carry = lax.fori_loop(0, half, functools.partial(sweep, finishing=False), (zero, zero), unroll=unroll)
    lax.fori_loop(half, n_chunks, functools.partial(sweep, finishing=True), carry, unroll=unroll)


def _hgrn2(u, lb, gn, consts):
    b, s, _ = u.shape
    mat_f, mat_b, mask_f, mask_b = consts

    def col(c0):
        return pl.BlockSpec((1, s, D_HEAD), lambda i, h: (i, 0, c0 + h))

    def whole(a):
        return pl.BlockSpec(a.shape, lambda i, h: (0,) * a.ndim)

    per_head = pl.BlockSpec((1, 1, D_HEAD), lambda i, h: (h, 0, 0))
    return pl.pallas_call(
        functools.partial(_hg_kernel, n_chunks=s // CHUNK),
        grid=(b, N_HEADS),
        in_specs=[col(COL_HGQ), col(COL_HGI), col(COL_HGFF), col(COL_HGFB), col(COL_HGG), per_head, per_head,
                  whole(mat_f), whole(mat_b), whole(mask_f), whole(mask_b)],
        out_specs=pl.BlockSpec((1, s, D_HEAD), lambda i, h: (i, 0, h)),
        out_shape=jax.ShapeDtypeStruct((b, s, N_HEADS * D_HEAD), BF16),
        scratch_shapes=[pltpu.VMEM((s, D_HEAD), F32), pltpu.VMEM((s, D_HEAD), F32)],
        compiler_params=pltpu.CompilerParams(dimension_semantics=("parallel", "parallel"),
                                             vmem_limit_bytes=VMEM_LIMIT),
        name="hgrn2",
    )(u, u, u, u, u, lb.reshape(N_HEADS, 1, D_HEAD), gn.reshape(N_HEADS, 1, D_HEAD),
      mat_f, mat_b, mask_f, mask_b)


def _merge_kernel(x_ref, yr_ref, yh_ref, za_ref, zb_ref, ga1_ref, sc2_ref, sh2_ref, gpost1_ref, gpre2_ref,
                  wr_ref, wh_ref, wo_ref, wrt_ref, x1_ref, h2_ref, aff_ref):
    br = jnp.dot(yr_ref[0], wr_ref[...], preferred_element_type=F32)
    bh = jnp.dot(yh_ref[0], wh_ref[...], preferred_element_type=F32)
    merged = _sigmoid(za_ref[0].astype(F32)) * br + _sigmoid(zb_ref[0].astype(F32)) * bh
    o = jnp.dot(merged.astype(BF16), wo_ref[...], preferred_element_type=F32)
    x1 = x_ref[0] + _rms(o) * (ga1_ref[0] * gpost1_ref[...])
    x1_ref[0] = x1
    h2 = _rms(x1) * (gpre2_ref[...] * (1.0 + sc2_ref[0])) + sh2_ref[0]
    h2_hi = h2.astype(BF16)
    h2_ref[0] = h2_hi
    h2_lo = (h2 - h2_hi.astype(F32)).astype(BF16)
    w_parts = wrt_ref[...]
    by_hi = lax.dot_general(w_parts, h2_hi, NT_DIMS, preferred_element_type=F32)
    logits = (by_hi[:N_EXPERTS] + by_hi[N_EXPERTS:]
              + lax.dot_general(w_parts[:N_EXPERTS], h2_lo, NT_DIMS, preferred_element_type=F32))
    ex = jnp.exp(logits - jnp.max(logits, axis=0, keepdims=True))
    aff_ref[...] = ex / jnp.sum(ex, axis=0, keepdims=True)


def _merge(x, y_ret, y_hg, u, ga1, sc2, sh2, g_post1, g_pre2, w_br_ret, w_br_hg, w_out, w_router_t, tm=512):
    b, s, d = x.shape
    nj = s // tm
    tok = lambda w: pl.BlockSpec((1, tm, w), lambda i, j: (i, j, 0))
    per_b = pl.BlockSpec((1, 1, d), lambda i, j: (i, 0, 0))
    vec = pl.BlockSpec((1, d), lambda i, j: (0, 0))
    whole = lambda a: pl.BlockSpec(a.shape, lambda i, j: (0,) * a.ndim)
    return pl.pallas_call(
        _merge_kernel,
        grid=(b, nj),
        in_specs=[tok(d), tok(N_HEADS * D_HEAD), tok(N_HEADS * D_HEAD),
                  pl.BlockSpec((1, tm, d), lambda i, j: (i, j, 0)),
                  pl.BlockSpec((1, tm, d), lambda i, j: (i, j, 1)),
                  per_b, per_b, per_b, vec, vec,
                  whole(w_br_ret), whole(w_br_hg), whole(w_out), whole(w_router_t)],
        out_specs=[tok(d), tok(d), pl.BlockSpec((N_EXPERTS, tm), lambda i, j: (0, i * nj + j))],
        out_shape=[jax.ShapeDtypeStruct((b, s, d), F32), jax.ShapeDtypeStruct((b, s, d), BF16),
                   jax.ShapeDtypeStruct((N_EXPERTS, b * s), F32)],
        compiler_params=pltpu.CompilerParams(dimension_semantics=("parallel", "parallel"),
                                             vmem_limit_bytes=VMEM_LIMIT),
        name="merge_outproj_router",
    )(x, y_ret, y_hg, u, u, ga1, sc2, sh2, g_post1.reshape(1, d), g_pre2.reshape(1, d),
      w_br_ret, w_br_hg, w_out, w_router_t)


TF_EXPERT = 512


N_FF_CHUNKS = D_EXPERT // TF_EXPERT


def _expert_kernel(total_ref, x_ref, wg_ref, wu_ref, wd_ref, o_ref, wg_bf, wu_bf, wd_bf, *, tm, n_tiles):
    s, j = pl.program_id(0), pl.program_id(1)
    fill, use = s % 2, 1 - s % 2
    for c in range(3 * N_FF_CHUNKS):
        @pl.when(jnp.logical_and(j == c, s < pl.num_programs(0) - 1))
        def _():
            which, f = divmod(c, N_FF_CHUNKS)
            src, dst = ((wg_ref, wg_bf), (wu_ref, wu_bf), (wd_ref, wd_bf))[which]
            dst[fill, f] = src[0].astype(BF16)

    live_rows = jnp.where(jnp.logical_and(s > 0, j < n_tiles), total_ref[jnp.maximum(s - 1, 0)] - j * tm, 0)
    half = tm // 2

    def swiglu(x):
        acc = jnp.zeros((x.shape[0], D_MODEL), F32)
        for f in range(N_FF_CHUNKS):
            a = jnp.dot(x, wg_bf[use, f], preferred_element_type=F32)
            up = jnp.dot(x, wu_bf[use, f], preferred_element_type=F32)
            acc += jnp.dot((a * _sigmoid(a) * up).astype(BF16), wd_bf[use, f], preferred_element_type=F32)
        return acc.astype(BF16)

    @pl.when(live_rows > half)
    def _():
        o_ref[0] = swiglu(x_ref[0])

    @pl.when(jnp.logical_and(live_rows > 0, live_rows <= half))
    def _():
        o_ref[0, :half, :] = swiglu(x_ref[0, :half, :])
        o_ref[0, half:, :] = jnp.zeros((half, D_MODEL), BF16)

    @pl.when(jnp.logical_and(live_rows <= 0, j < n_tiles))
    def _():
        o_ref[0] = jnp.zeros((tm, D_MODEL), BF16)


def _experts(total, xe, wg, wu, wd, tm=512):
    e, rows, d = xe.shape
    n_tiles = rows // tm
    steps = max(n_tiles, 3 * N_FF_CHUNKS)
    nxt = lambda s: jnp.minimum(s, e - 1)
    cur = lambda s: jnp.maximum(s - 1, 0)
    tile = lambda j: jnp.minimum(j, n_tiles - 1)
    chunk = lambda j, which: jnp.clip(j - which * N_FF_CHUNKS, 0, N_FF_CHUNKS - 1)
    return pl.pallas_call(
        functools.partial(_expert_kernel, tm=tm, n_tiles=n_tiles),
        grid_spec=pltpu.PrefetchScalarGridSpec(
            num_scalar_prefetch=1,
            grid=(e + 1, steps),
            in_specs=[pl.BlockSpec((1, tm, d), lambda s, j, t: (cur(s), tile(j), 0)),
                      pl.BlockSpec((1, d, TF_EXPERT), lambda s, j, t: (nxt(s), 0, chunk(j, 0))),
                      pl.BlockSpec((1, d, TF_EXPERT), lambda s, j, t: (nxt(s), 0, chunk(j, 1))),
                      pl.BlockSpec((1, TF_EXPERT, d), lambda s, j, t: (nxt(s), chunk(j, 2), 0))],
            out_specs=pl.BlockSpec((1, tm, d), lambda s, j, t: (jnp.where(s == 0, e, s - 1), tile(j), 0)),
            scratch_shapes=[pltpu.VMEM((2, N_FF_CHUNKS, d, TF_EXPERT), BF16),
                            pltpu.VMEM((2, N_FF_CHUNKS, d, TF_EXPERT), BF16),
                            pltpu.VMEM((2, N_FF_CHUNKS, TF_EXPERT, d), BF16)]),
        out_shape=jax.ShapeDtypeStruct((e + 1, rows, d), BF16),
        compiler_params=pltpu.CompilerParams(dimension_semantics=("arbitrary", "arbitrary"),
                                             vmem_limit_bytes=VMEM_LIMIT),
        name="expert_ffn",
    )(total, xe, wg, wu, wd)


TT = 512
WIN = 128
SEG_ALIGN = 16
NOT_CHOSEN = -float(1 << 20)


MANTISSA_STEPS = 53


def _threshold_kernel(a_ref, o_ref, *, cap):
    def count_ge(x):
        return jnp.sum(jnp.where(a_ref[...] >= x, 1.0, 0.0), axis=1, keepdims=True)

    lo = jnp.where(count_ge(MIN_NORMAL) >= cap, MIN_NORMAL, 0.0) * jnp.ones((N_EXPERTS, 1), F32)
    for j in (64, 32, 16, 8, 4, 2, 1):
        cand = lo * (2.0 ** j)
        lo = jnp.where(count_ge(cand) >= cap, cand, lo)
    hi = jnp.where(lo > 0.0, 2.0 * lo, MIN_NORMAL)

    def halve(_, bounds):
        lo, hi = bounds
        mid = 0.5 * (lo + hi)
        ok = count_ge(mid) >= cap
        return jnp.where(ok, mid, lo), jnp.where(ok, hi, mid)

    lo, hi = lax.fori_loop(0, MANTISSA_STEPS, halve, (lo, hi))
    a = a_ref[...]
    thr = jnp.min(jnp.where(a >= lo, a, 2.0), axis=1, keepdims=True)
    n_gt = jnp.sum(jnp.where(a > thr, 1.0, 0.0), axis=1, keepdims=True)
    o_ref[0] = jnp.broadcast_to(thr, (N_EXPERTS, LANES))
    o_ref[1] = jnp.broadcast_to(cap - n_gt, (N_EXPERTS, LANES))


def _thresholds(aff_t, cap):
    return pl.pallas_call(
        functools.partial(_threshold_kernel, cap=cap),
        out_shape=jax.ShapeDtypeStruct((2, N_EXPERTS, LANES), F32),
        compiler_params=pltpu.CompilerParams(vmem_limit_bytes=VMEM_LIMIT),
        name="route_threshold",
    )(aff_t)


def _select_kernel(a_ref, thr_ref, upper_ref, slot_t_ref, slot_ref, gate_ref, cnt_ref, seen_ref):
    @pl.when(pl.program_id(0) == 0)
    def _():
        seen_ref[...] = jnp.zeros_like(seen_ref)

    a = a_ref[...]
    thr = thr_ref[0][:, 0:1]
    need = thr_ref[1][:, 0:1]
    gt, eq = a > thr, a == thr
    marks = jnp.concatenate([jnp.where(gt, 1.0, 0.0), jnp.where(eq, 1.0, 0.0)], axis=0).astype(BF16)
    before = jnp.dot(marks, upper_ref[...], preferred_element_type=F32)
    gt_before, eq_before = before[:N_EXPERTS], before[N_EXPERTS:]
    seen = seen_ref[:, 0:1]
    eq_rank = seen + eq_before
    sel = jnp.logical_or(gt, jnp.logical_and(eq, eq_rank < need))
    rank = gt_before + jnp.minimum(eq_rank, need) - jnp.minimum(seen, need)
    slot_t = jnp.where(sel, rank, NOT_CHOSEN)
    slot_t_ref[...] = slot_t.astype(jnp.int32)
    cnt_ref[0] = jnp.broadcast_to(jnp.sum(jnp.where(sel, 1, 0), axis=1, keepdims=True), (N_EXPERTS, LANES))
    seen_ref[...] = seen_ref[...] + jnp.sum(jnp.where(eq, 1.0, 0.0), axis=1, keepdims=True)
    pad = jnp.zeros((LANES - N_EXPERTS, TT), F32)
    slot_ref[...] = jnp.concatenate([slot_t, pad], axis=0).T[:, :N_EXPERTS].astype(jnp.int32)
    gate_ref[...] = jnp.concatenate([a, pad], axis=0).T[:, :N_EXPERTS]


def _select(aff_t, thr):
    n_tok = aff_t.shape[1]
    n_tiles = n_tok // TT
    t = np.arange(TT)
    upper = jnp.asarray(t[:, None] < t[None, :], BF16)
    return pl.pallas_call(
        _select_kernel,
        grid=(n_tiles,),
        in_specs=[pl.BlockSpec((N_EXPERTS, TT), lambda i: (0, i)),
                  pl.BlockSpec((2, N_EXPERTS, LANES), lambda i: (0, 0, 0)),
                  pl.BlockSpec((TT, TT), lambda i: (0, 0))],
        out_specs=[pl.BlockSpec((N_EXPERTS, TT), lambda i: (0, i)),
                   pl.BlockSpec((TT, N_EXPERTS), lambda i: (i, 0)),
                   pl.BlockSpec((TT, N_EXPERTS), lambda i: (i, 0)),
                   pl.BlockSpec((1, N_EXPERTS, LANES), lambda i: (i, 0, 0))],
        out_shape=[jax.ShapeDtypeStruct((N_EXPERTS, n_tok), jnp.int32),
                   jax.ShapeDtypeStruct((n_tok, N_EXPERTS), jnp.int32),
                   jax.ShapeDtypeStruct((n_tok, N_EXPERTS), F32),
                   jax.ShapeDtypeStruct((n_tiles, N_EXPERTS, LANES), jnp.int32)],
        scratch_shapes=[pltpu.VMEM((N_EXPERTS, LANES), F32)],
        compiler_params=pltpu.CompilerParams(dimension_semantics=("arbitrary",)),
        name="route_select",
    )(aff_t, thr, upper)


def _dispatch_kernel(start_ref, off_ref, cnt_ref, slot_ref, h_ref, xe_in_ref, xe_ref,
                     stage_ref, extra_ref, carry_ref, sem, extra_sem):
    del xe_in_ref
    i = pl.program_id(0)
    par = i % 2
    tiles_per_win = WIN // SEG_ALIGN

    @pl.when(i == 0)
    def _():
        carry_ref[...] = jnp.zeros_like(carry_ref)

    row = lax.broadcasted_iota(jnp.int32, (WIN, TT), 0)
    slot = slot_ref[...]
    onehot = jnp.concatenate([jnp.where(slot[e:e + 1, :] + off_ref[i * N_EXPERTS + e] == row, 1.0, 0.0).astype(BF16)
                              for e in range(N_EXPERTS)], axis=0)
    stage_ref[par] = jnp.dot(onehot, h_ref[...], preferred_element_type=F32).astype(BF16)

    def window_copy(e):
        start = pl.multiple_of(start_ref[i * N_EXPERTS + e], SEG_ALIGN)
        return pltpu.make_async_copy(stage_ref.at[par, pl.ds(e * WIN, WIN), :],
                                     xe_ref.at[e, pl.ds(start, WIN), :], sem.at[e])

    for e in range(N_EXPERTS):
        end = off_ref[i * N_EXPERTS + e] + cnt_ref[i * N_EXPERTS + e]
        head = pl.ds(e * WIN, SEG_ALIGN)
        stage_ref[par, head, :] = stage_ref[par, head, :] + carry_ref[e]
        tile = jnp.minimum(end // SEG_ALIGN, tiles_per_win - 1)
        last = stage_ref[par, pl.ds(pl.multiple_of(e * WIN + tile * SEG_ALIGN, SEG_ALIGN), SEG_ALIGN), :]
        carry_ref[e] = jnp.where(end < WIN, last, jnp.zeros_like(last))

        @pl.when(i > 0)
        def _():
            window_copy(e).wait()

        window_copy(e).start()

    def extra_windows(e, carry):
        k = i * N_EXPERTS + e
        end = off_ref[k] + cnt_ref[k]

        def one(w, c):
            first = (w + 1) * WIN
            srow = slot_ref[pl.ds(e, 1), :] + (off_ref[k] - first)
            oh = jnp.where(srow == row, 1.0, 0.0).astype(BF16)
            extra_ref[...] = jnp.dot(oh, h_ref[...], preferred_element_type=F32).astype(BF16)
            cp = pltpu.make_async_copy(
                extra_ref, xe_ref.at[e, pl.ds(pl.multiple_of(start_ref[k] + first, SEG_ALIGN), WIN), :], extra_sem)
            cp.start()
            tile = end // SEG_ALIGN - (w + 1) * tiles_per_win
            inside = jnp.logical_and(tile >= 0, tile < tiles_per_win)
            last = extra_ref[pl.ds(pl.multiple_of(jnp.clip(tile, 0, tiles_per_win - 1) * SEG_ALIGN, SEG_ALIGN),
                                   SEG_ALIGN), :]
            carry_ref[e] = jnp.where(inside, last, carry_ref[e])
            cp.wait()
            return c

        return lax.fori_loop(0, jnp.maximum(end - 1, 0) // WIN, one, carry)

    lax.fori_loop(0, N_EXPERTS, extra_windows, 0)

    @pl.when(i == pl.num_programs(0) - 1)
    def _():
        for e in range(N_EXPERTS):
            window_copy(e).wait()


def _dispatch(start, off, cnt, slot_t, h2, xe0):
    n_tok, d = h2.shape
    rows = xe0.shape[1]
    return pl.pallas_call(
        _dispatch_kernel,
        grid_spec=pltpu.PrefetchScalarGridSpec(
            num_scalar_prefetch=3,
            grid=(n_tok // TT,),
            in_specs=[pl.BlockSpec((N_EXPERTS, TT), lambda i, *_: (0, i)),
                      pl.BlockSpec((TT, d), lambda i, *_: (i, 0)),
                      pl.BlockSpec(memory_space=pl.ANY)],
            out_specs=pl.BlockSpec(memory_space=pl.ANY),
            scratch_shapes=[pltpu.VMEM((2, N_EXPERTS * WIN, d), BF16), pltpu.VMEM((WIN, d), BF16),
                            pltpu.VMEM((N_EXPERTS, SEG_ALIGN, d), BF16),
                            pltpu.SemaphoreType.DMA((N_EXPERTS,)), pltpu.SemaphoreType.DMA(())]),
        out_shape=jax.ShapeDtypeStruct((N_EXPERTS, rows, d), BF16),
        input_output_aliases={5: 0},
        compiler_params=pltpu.CompilerParams(dimension_semantics=("arbitrary",), vmem_limit_bytes=VMEM_LIMIT),
        name="route_dispatch",
    )(start, off, cnt, slot_t, h2, xe0)


def _combine_kernel(start_ref, off_ref, cnt_ref, slot_ref, gate_ref, x1_ref, ga2_ref, g_ref, ye_ref, o_ref,
                    ybuf, extra_ref, acc_ref, sem, extra_sem):
    i = pl.program_id(0)
    par = i % 2

    def window_copy(step, e, buf):
        start = pl.multiple_of(start_ref[step * N_EXPERTS + e], SEG_ALIGN)
        return pltpu.make_async_copy(ye_ref.at[e, pl.ds(start, WIN), :],
                                     ybuf.at[buf, pl.ds(e * WIN, WIN), :], sem.at[buf])

    @pl.when(i == 0)
    def _():
        for e in range(N_EXPERTS):
            window_copy(0, e, 0).start()

    @pl.when(i + 1 < pl.num_programs(0))
    def _():
        for e in range(N_EXPERTS):
            window_copy(i + 1, e, 1 - par).start()

    slot, gate = slot_ref[...], gate_ref[...]
    lane = lax.broadcasted_iota(jnp.int32, (TT, WIN), 1)
    weights = jnp.concatenate([jnp.where(slot[:, e:e + 1] + off_ref[i * N_EXPERTS + e] == lane,
                                         gate[:, e:e + 1], 0.0).astype(BF16)
                               for e in range(N_EXPERTS)], axis=1)
    for e in range(N_EXPERTS):
        window_copy(i, e, par).wait()
    moe = jnp.dot(weights, ybuf[par], preferred_element_type=F32)
    n_extra = sum(jnp.maximum(off_ref[i * N_EXPERTS + e] + cnt_ref[i * N_EXPERTS + e] - 1, 0) // WIN
                  for e in range(N_EXPERTS))

    def finish(mixture):
        o_ref[...] = x1_ref[...] + ga2_ref[0] * (_rms(mixture) * g_ref[...])

    @pl.when(n_extra == 0)
    def _():
        finish(moe)

    def extra_windows(e, carry):
        k = i * N_EXPERTS + e

        def one(w, c):
            first = (w + 1) * WIN
            cp = pltpu.make_async_copy(
                ye_ref.at[e, pl.ds(pl.multiple_of(start_ref[k] + first, SEG_ALIGN), WIN), :], extra_ref, extra_sem)
            cp.start()
            pick = lax.broadcasted_iota(jnp.int32, (TT, N_EXPERTS), 1) == e
            slot_e = jnp.sum(jnp.where(pick, slot_ref[...], 0), axis=1, keepdims=True)
            gate_e = jnp.sum(jnp.where(pick, gate_ref[...], 0.0), axis=1, keepdims=True)
            wts = jnp.where(slot_e + (off_ref[k] - first) == lane, gate_e, 0.0).astype(BF16)
            cp.wait()
            acc_ref[...] += jnp.dot(wts, extra_ref[...], preferred_element_type=F32)
            return c

        return lax.fori_loop(0, jnp.maximum(off_ref[k] + cnt_ref[k] - 1, 0) // WIN, one, carry)

    @pl.when(n_extra > 0)
    def _():
        acc_ref[...] = moe
        lax.fori_loop(0, N_EXPERTS, extra_windows, 0)
        finish(acc_ref[...])


def _combine(start, off, cnt, slot, gate, x1, ga2, g_post2, ye, tiles_per_seq):
    n_tok, d = x1.shape
    tok = lambda w: pl.BlockSpec((TT, w), lambda i, *_: (i, 0))
    return pl.pallas_call(
        _combine_kernel,
        grid_spec=pltpu.PrefetchScalarGridSpec(
            num_scalar_prefetch=3,
            grid=(n_tok // TT,),
            in_specs=[tok(N_EXPERTS), tok(N_EXPERTS), tok(d),
                      pl.BlockSpec((1, 1, d), lambda i, *_: (i // tiles_per_seq, 0, 0)),
                      pl.BlockSpec((1, d), lambda i, *_: (0, 0)),
                      pl.BlockSpec(memory_space=pl.ANY)],
            out_specs=tok(d),
            scratch_shapes=[pltpu.VMEM((2, N_EXPERTS * WIN, d), BF16), pltpu.VMEM((WIN, d), BF16),
                            pltpu.VMEM((TT, d), F32),
                            pltpu.SemaphoreType.DMA((2,)), pltpu.SemaphoreType.DMA(())]),
        out_shape=jax.ShapeDtypeStruct((n_tok, d), F32),
        compiler_params=pltpu.CompilerParams(dimension_semantics=("arbitrary",), vmem_limit_bytes=VMEM_LIMIT),
        name="route_combine_final",
    )(start, off, cnt, slot, gate, x1, ga2, g_post2.reshape(1, d), ye)


def _moe(groups, g_post2, wg, wu, wd, tm_expert=512):
    d = D_MODEL
    routed = []
    used = 0
    for x1, h2, aff_t, ga2 in groups:
        b, s, _ = x1.shape
        n_tok = b * s
        cap = CAPACITY_FACTOR * n_tok // N_EXPERTS
        assert cap % SEG_ALIGN == 0
        slot_t, slot, gate, cnt = _select(aff_t, _thresholds(aff_t, cap))
        cnt = cnt[:, :, 0]
        base = jnp.cumsum(cnt, axis=0) - cnt + used
        start = base // SEG_ALIGN * SEG_ALIGN
        flat = lambda a: a.reshape(-1).astype(jnp.int32)
        routed.append((flat(start), flat(base - start), flat(cnt), slot_t, slot, gate))
        used += cap
    rows = -(-(used + WIN) // tm_expert) * tm_expert
    xe = jnp.zeros((N_EXPERTS, rows, d), BF16)
    for (x1, h2, _, _), (start, off, cnt, slot_t, _, _) in zip(groups, routed):
        xe = _dispatch(start, off, cnt, slot_t, h2.reshape(-1, d), xe)
    ye = _experts(jnp.full((N_EXPERTS,), used, jnp.int32), xe, wg, wu, wd, tm=tm_expert)
    outs = []
    for (x1, _, _, ga2), (start, off, cnt, _, slot, gate) in zip(groups, routed):
        b, s, _ = x1.shape
        outs.append(_combine(start, off, cnt, slot, gate, x1.reshape(-1, d), ga2, g_post2, ye, s // TT)
                    .reshape(b, s, d))
    return outs


def _hi_lo_rows(w):
    hi = w.astype(BF16)
    return jnp.concatenate([hi, (w - hi.astype(F32)).astype(BF16)], axis=0)


def _rope_tables(s):
    half = D_HEAD // 2
    inv = 1.0 / (ROPE_BASE ** (jnp.arange(half, dtype=F32) / half))
    ang = jnp.arange(s, dtype=F32)[:, None] * inv[None, :]
    cos, sin = jnp.cos(ang), jnp.sin(ang)
    return jnp.concatenate([cos, cos], axis=-1), jnp.concatenate([-sin, sin], axis=-1)


def _token_mixer(x, mod, p):
    b, s, d = x.shape
    sh1, sc1, ga1, sh2, sc2, ga2 = [m[:, None, :] for m in jnp.split(mod, 6, axis=-1)]
    cos2, sin2 = _rope_tables(s)
    u = _inproj(x, sc1, sh1, p["g_pre1"], cos2, sin2, p["w_in"])
    y_ret = _retention(u, p["log_gamma"], p["gn_ret"])
    y_hg = _hgrn2(u, p["lb"], p["gn_hg"], p["hg_consts"])
    x1, h2, aff_t = _merge(x, y_ret, y_hg, u, ga1, sc2, sh2, p["g_post1"], p["g_pre2"],
                           p["w_br_ret"], p["w_br_hg"], p["w_out"], p["w_router_t"])
    return x1, h2, aff_t, ga2


def kernel(x_prompt, x_sample, c_prompt, c_sample, w_ada, b_ada, g_pre1, g_post1, w_in, gn_ret, gn_hg, lb_logits, w_br_ret, w_br_hg, w_out, g_pre2, g_post2, w_router, w_gate, w_up, w_down):
    lower_bounds = jnp.cumsum(jax.nn.softmax(lb_logits.astype(F32), axis=0), axis=0)
    log_gamma = jnp.log(1.0 - jnp.power(2.0, -5.0 - jnp.arange(N_HEADS, dtype=F32)))
    consts = _hg_constants()
    n_gate = 2 * D_MODEL
    xp, xs = x_prompt, x_sample
    bp, bs = xp.shape[0], xs.shape[0]
    for l in range(w_ada.shape[0]):
        c_all = jnp.concatenate([c_prompt, c_sample, jnp.zeros((-(bp + bs) % 8, D_MODEL), F32)], axis=0)
        mod = _ada(c_all, w_ada[l], b_ada[l])
        w_in_l = w_in[l]
        p = dict(
            g_pre1=g_pre1[l], g_post1=g_post1[l], g_pre2=g_pre2[l], g_post2=g_post2[l],
            w_in=jnp.concatenate([w_in_l[:, -n_gate:], w_in_l[:, :-n_gate]], axis=1).astype(BF16),
            log_gamma=jnp.broadcast_to(log_gamma[:, None, None], (N_HEADS, 1, LANES)),
            gn_ret=gn_ret[l], gn_hg=gn_hg[l], lb=lower_bounds[l], hg_consts=consts,
            w_br_ret=w_br_ret[l].astype(BF16), w_br_hg=w_br_hg[l].astype(BF16), w_out=w_out[l].astype(BF16),
            w_router_t=_hi_lo_rows(w_router[l].T),
            w_gate=w_gate[l], w_up=w_up[l], w_down=w_down[l],
        )
        mixed = [_token_mixer(xp, mod[:bp], p), _token_mixer(xs, mod[bp:bp + bs], p)]
        xp, xs = _moe(mixed, p["g_post2"], p["w_gate"], p["w_up"], p["w_down"])
    return (xp, xs)
```
